```python
import jax, jax.numpy as jnp
from jax import lax
import numpy as np

D_MODEL = 1024
BATCH = 8
SEQ = 4096
DEPTH = 1

PLE_DIM = 256
RMS_EPS = 1e-6
MACARON = 0.5
FFN_HIDDEN = 2816
ATT_HEADS = 8
ATT_KV_HEADS = 2
ATT_HEAD_DIM = 64
ATT_WINDOW = 128
ATT_BLOCK = 128
ROPE_THETA = 500000.0
ROPE_DIM = ATT_HEAD_DIM // 4
MLSTM_HEADS = 4
MLSTM_QK_DIM = 128
MLSTM_V_DIM = 256
MLSTM_CHUNK = 64
MLSTM_CONV = 5
ATT_Q_W = ATT_HEADS * ATT_HEAD_DIM
ATT_KV_W = ATT_KV_HEADS * ATT_HEAD_DIM
M_QK_W = MLSTM_HEADS * MLSTM_QK_DIM
M_V_W = MLSTM_HEADS * MLSTM_V_DIM
M_GATE_W = 4 * MLSTM_HEADS
MERGE_W = 2 * D_MODEL
IN_SIZES = (ATT_Q_W, ATT_KV_W, ATT_KV_W, M_QK_W, M_QK_W, M_V_W, M_V_W, M_GATE_W, MERGE_W)
IN_WIDTH = ATT_Q_W + 2 * ATT_KV_W + 2 * M_QK_W + 2 * M_V_W + M_GATE_W + MERGE_W

kernel_name = 'hybrid_swa_mlstm_macaron_encoder'


def rms_norm(x, g):
    xf = x.astype(jnp.float32)
    y = xf * lax.rsqrt(jnp.mean(xf * xf, axis=-1, keepdims=True) + RMS_EPS)
    return (y * g.astype(jnp.float32)).astype(x.dtype)


def swiglu(x, w1, w2):
    z = x @ w1
    return (jax.nn.silu(z[..., :FFN_HIDDEN]) * z[..., FFN_HIDDEN:]) @ w2


def split_cols(z, sizes):
    outs = []
    off = 0
    for s in sizes:
        outs.append(z[..., off:off + s])
        off += s
    return outs


def rope_tables(seq):
    pos = jnp.arange(seq, dtype=jnp.float32)
    inv = ROPE_THETA ** (-jnp.arange(0, ROPE_DIM, 2, dtype=jnp.float32) / ROPE_DIM)
    ang = pos[:, None] * inv[None, :]
    return jnp.cos(ang)[:, None, :], jnp.sin(ang)[:, None, :]


def partial_rope(x, cos, sin):
    half = ROPE_DIM // 2
    xr = x[..., :ROPE_DIM].astype(jnp.float32)
    x1, x2 = xr[..., :half], xr[..., half:]
    rot = jnp.concatenate([x1 * cos - x2 * sin, x2 * cos + x1 * sin], axis=-1)
    return jnp.concatenate([rot.astype(x.dtype), x[..., ROPE_DIM:]], axis=-1)


def windowed_gqa(q, k, v, sink):
    b, s, hq, dh = q.shape
    g = hq // ATT_KV_HEADS
    blk = ATT_BLOCK
    nb = s // blk
    pad = ((0, 0), (blk, blk), (0, 0), (0, 0))
    kb = jnp.pad(k, pad).reshape(b, nb + 2, blk, ATT_KV_HEADS, dh)
    vb = jnp.pad(v, pad).reshape(b, nb + 2, blk, ATT_KV_HEADS, dh)
    kw = jnp.concatenate([kb[:, :-2], kb[:, 1:-1], kb[:, 2:]], axis=2)
    vw = jnp.concatenate([vb[:, :-2], vb[:, 1:-1], vb[:, 2:]], axis=2)
    qb = q.reshape(b, nb, blk, ATT_KV_HEADS, g, dh)
    sc = jnp.einsum('bnqhgd,bnkhd->bnhgqk', qb, kw).astype(jnp.float32) * (dh ** -0.5)
    qi = jnp.arange(blk)[:, None]
    kc = jnp.arange(3 * blk)[None, :]
    rel = kc - blk - qi
    kpos = jnp.arange(nb)[:, None, None] * blk - blk + kc[None]
    mask = (jnp.abs(rel) <= ATT_WINDOW)[None] & (kpos >= 0) & (kpos < s)
    sc = jnp.where(mask[None, :, None, None], sc, -jnp.inf)
    sk = sink.astype(jnp.float32).reshape(1, 1, ATT_KV_HEADS, g, 1, 1)
    mx = jnp.maximum(jnp.max(sc, axis=-1, keepdims=True), sk)
    pr = jnp.exp(sc - mx)
    pr = pr / (jnp.sum(pr, axis=-1, keepdims=True) + jnp.exp(sk - mx))
    o = jnp.einsum('bnhgqk,bnkhd->bnqhgd', pr.astype(v.dtype), vw)
    return o.reshape(b, s, hq * dh)


def centred_depthwise_conv(x, w, bias):
    c = x.shape[-1]
    y = lax.conv_general_dilated(
        x, w[:, None, :].astype(x.dtype), window_strides=(1,),
        padding=((MLSTM_CONV // 2, MLSTM_CONV // 2),),
        dimension_numbers=('NWC', 'WIO', 'NWC'), feature_group_count=c)
    return y + bias


def mlstm_chunkwise(q, k, v, log_i, log_f):
    n, s, h, dk = q.shape
    dv = v.shape[-1]
    L = MLSTM_CHUNK
    nc = s // L

    def to_chunks(t):
        return jnp.moveaxis(t.reshape((n, nc, L, h) + t.shape[3:]), (1, 3), (0, 2))

    xs = (to_chunks(q), to_chunks(k), to_chunks(v), to_chunks(log_i), to_chunks(log_f))
    tril = jnp.tril(jnp.ones((L, L), dtype=bool))

    def step(carry, inp):
        c_mat, n_vec, m_prev = carry
        qc, kc, vc, li, lf = inp
        b = jnp.cumsum(lf, axis=-1)
        d_log = jnp.where(tril, b[..., :, None] - b[..., None, :] + li[..., None, :], -jnp.inf)
        inter = b + m_prev[..., None]
        m_row = jnp.maximum(inter, jnp.max(d_log, axis=-1))
        w_intra = jnp.exp(d_log - m_row[..., None])
        w_inter = jnp.exp(inter - m_row)
        sc = jnp.einsum('nhjd,nhtd->nhjt', qc, kc) * w_intra
        num = jnp.einsum('nhjt,nhtv->nhjv', sc, vc) + w_inter[..., None] * jnp.einsum('nhjd,nhdv->nhjv', qc, c_mat)
        den = jnp.sum(sc, axis=-1) + w_inter * jnp.einsum('nhjd,nhd->nhj', qc, n_vec)
        h_out = num / jnp.maximum(jnp.abs(den), jnp.exp(-m_row))[..., None]
        b_last = b[..., -1]
        k_log = b_last[..., None] - b + li
        m_new = jnp.maximum(b_last + m_prev, jnp.max(k_log, axis=-1))
        w_k = jnp.exp(k_log - m_new[..., None])
        decay = jnp.exp(b_last + m_prev - m_new)
        c_new = decay[..., None, None] * c_mat + jnp.einsum('nht,nhtd,nhtv->nhdv', w_k, kc, vc)
        n_new = decay[..., None] * n_vec + jnp.einsum('nht,nhtd->nhd', w_k, kc)
        return (c_new, n_new, m_new), h_out

    init = (jnp.zeros((n, h, dk, dv), jnp.float32), jnp.zeros((n, h, dk), jnp.float32), jnp.zeros((n, h), jnp.float32))
    _, hs = lax.scan(step, init, xs)
    return jnp.moveaxis(hs, (0, 2), (1, 3)).reshape(n, s, h, dv)


def mlstm_bidirectional(q, k, v, li_fw, lf_fw, li_bw, lf_bw):
    nb = q.shape[0]

    def both(a, c):
        return jnp.concatenate([a, jnp.flip(c, axis=1)], axis=0)

    hs = mlstm_chunkwise(both(q, q), both(k, k), both(v, v), both(li_fw, li_bw), both(lf_fw, lf_bw))
    return hs[:nb] + jnp.flip(hs[nb:], axis=1)


def token_mixer(u, w_in, b_gates, conv_w, conv_b, attn_sink, mlstm_norm, w_branch_attn, w_branch_mlstm, w_out):
    bsz, s, _ = u.shape
    z = u @ w_in
    aq, ak, av, mq, mk, mv, mo, mg, bg = split_cols(z, IN_SIZES)
    cos, sin = rope_tables(s)
    aq = partial_rope(aq.reshape(bsz, s, ATT_HEADS, ATT_HEAD_DIM), cos, sin)
    ak = partial_rope(ak.reshape(bsz, s, ATT_KV_HEADS, ATT_HEAD_DIM), cos, sin)
    av = av.reshape(bsz, s, ATT_KV_HEADS, ATT_HEAD_DIM)
    ya = windowed_gqa(aq, ak, av, attn_sink) @ w_branch_attn
    mqk = jax.nn.silu(centred_depthwise_conv(jnp.concatenate([mq, mk], axis=-1), conv_w, conv_b))
    mq = mqk[..., :M_QK_W].reshape(bsz, s, MLSTM_HEADS, MLSTM_QK_DIM)
    mk = mqk[..., M_QK_W:].reshape(bsz, s, MLSTM_HEADS, MLSTM_QK_DIM) * (MLSTM_QK_DIM ** -0.5)
    mv = mv.reshape(bsz, s, MLSTM_HEADS, MLSTM_V_DIM)
    gates = (mg + b_gates).astype(jnp.float32).reshape(bsz, s, 4, MLSTM_HEADS)
    li_fw = gates[:, :, 0]
    lf_fw = jax.nn.log_sigmoid(gates[:, :, 1])
    li_bw = gates[:, :, 2]
    lf_bw = jax.nn.log_sigmoid(gates[:, :, 3])
    hm = mlstm_bidirectional(mq, mk, mv, li_fw, lf_fw, li_bw, lf_bw)
    hm = hm * lax.rsqrt(jnp.mean(hm * hm, axis=-1, keepdims=True) + RMS_EPS)
    hm = hm * mlstm_norm.astype(jnp.float32).reshape(MLSTM_HEADS, MLSTM_V_DIM)
    hm = (hm.reshape(bsz, s, M_V_W) * jax.nn.sigmoid(mo.astype(jnp.float32))).astype(u.dtype)
    ym = hm @ w_branch_mlstm
    merged = jax.nn.sigmoid(bg[..., :D_MODEL]) * ya + jax.nn.sigmoid(bg[..., D_MODEL:]) * ym
    return merged @ w_out


def setup_inputs(seed: int = 0) -> dict:
    key = jax.random.key(seed)
    ks = jax.random.split(key, 25)
    f32 = jnp.float32

    def w(k, shape, fan_in):
        return jax.random.normal(k, shape, f32) * (fan_in ** -0.5)

    def gain(k, width):
        return 1.0 + 0.05 * jax.random.normal(k, (DEPTH, width), f32)

    f_off = jnp.linspace(3.0, 6.0, MLSTM_HEADS)
    i_off = jnp.zeros((MLSTM_HEADS,), f32)
    gate_off = jnp.stack([i_off, f_off, i_off, f_off])
    b_gates = (0.1 * jax.random.normal(ks[8], (DEPTH, 4, MLSTM_HEADS), f32) + gate_off[None]).reshape(DEPTH, M_GATE_W)
    return {
        'x': jax.random.normal(ks[0], (BATCH, SEQ, D_MODEL), f32),
        'p': jax.random.normal(ks[1], (DEPTH, BATCH, SEQ, PLE_DIM), f32),
        'ffn1_norm_pre': gain(ks[2], D_MODEL),
        'ffn1_w1': w(ks[3], (DEPTH, D_MODEL, 2 * FFN_HIDDEN), D_MODEL),
        'ffn1_w2': w(ks[4], (DEPTH, FFN_HIDDEN, D_MODEL), FFN_HIDDEN),
        'ffn1_norm_post': gain(ks[5], D_MODEL),
        'mix_norm_pre': gain(ks[6], D_MODEL),
        'w_in': w(ks[7], (DEPTH, D_MODEL, IN_WIDTH), D_MODEL),
        'b_gates': b_gates,
        'conv_w': w(ks[9], (DEPTH, MLSTM_CONV, 2 * M_QK_W), MLSTM_CONV),
        'conv_b': 0.02 * jax.random.normal(ks[10], (DEPTH, 2 * M_QK_W), f32),
        'attn_sink': 0.5 * jax.random.normal(ks[11], (DEPTH, ATT_HEADS), f32),
        'mlstm_norm': gain(ks[12], M_V_W),
        'w_branch_attn': w(ks[13], (DEPTH, ATT_Q_W, D_MODEL), ATT_Q_W),
        'w_branch_mlstm': w(ks[14], (DEPTH, M_V_W, D_MODEL), M_V_W),
        'w_out': w(ks[15], (DEPTH, D_MODEL, D_MODEL), D_MODEL),
        'mix_norm_post': gain(ks[16], D_MODEL),
        'ffn2_norm_pre': gain(ks[17], D_MODEL),
        'ffn2_w1': w(ks[18], (DEPTH, D_MODEL, 2 * FFN_HIDDEN), D_MODEL),
        'ffn2_w2': w(ks[19], (DEPTH, FFN_HIDDEN, D_MODEL), FFN_HIDDEN),
        'ffn2_norm_post': gain(ks[20], D_MODEL),
        'ple_norm_pre': gain(ks[21], D_MODEL),
        'w_ple_gate': w(ks[22], (DEPTH, D_MODEL, D_MODEL), D_MODEL),
        'w_ple_proj': w(ks[23], (DEPTH, PLE_DIM, D_MODEL), PLE_DIM),
        'ple_norm_post': gain(ks[24], D_MODEL),
    }


def reference(x, p, ffn1_norm_pre, ffn1_w1, ffn1_w2, ffn1_norm_post, mix_norm_pre, w_in, b_gates, conv_w, conv_b,
              attn_sink, mlstm_norm, w_branch_attn, w_branch_mlstm, w_out, mix_norm_post, ffn2_norm_pre, ffn2_w1,
              ffn2_w2, ffn2_norm_post, ple_norm_pre, w_ple_gate, w_ple_proj, ple_norm_post):
    h = x
    for i in range(DEPTH):
        h = h + MACARON * rms_norm(swiglu(rms_norm(h, ffn1_norm_pre[i]), ffn1_w1[i], ffn1_w2[i]), ffn1_norm_post[i])
        mix = token_mixer(rms_norm(h, mix_norm_pre[i]), w_in[i], b_gates[i], conv_w[i], conv_b[i], attn_sink[i],
                          mlstm_norm[i], w_branch_attn[i], w_branch_mlstm[i], w_out[i])
        h = h + rms_norm(mix, mix_norm_post[i])
        h = h + MACARON * rms_norm(swiglu(rms_norm(h, ffn2_norm_pre[i]), ffn2_w1[i], ffn2_w2[i]), ffn2_norm_post[i])
        gate = jax.nn.sigmoid(rms_norm(h, ple_norm_pre[i]) @ w_ple_gate[i])
        h = h + rms_norm((p[i] @ w_ple_proj[i]) * gate, ple_norm_post[i])
    return h
```

```python
import functools

import jax
import jax.numpy as jnp
from jax import lax
from jax.experimental import pallas as pl
from jax.experimental.pallas import tpu as pltpu

D_MODEL = 1024
PLE_DIM = 256
RMS_EPS = 1e-6
MACARON = 0.5
FFN_HIDDEN = 2816
ATT_HEADS = 8
ATT_KV_HEADS = 2
ATT_HEAD_DIM = 64
ATT_WINDOW = 128
ATT_BLOCK = 128
ROPE_THETA = 500000.0
ROPE_DIM = ATT_HEAD_DIM // 4
MLSTM_HEADS = 4
MLSTM_QK_DIM = 128
MLSTM_V_DIM = 256
MLSTM_CONV = 5
ATT_Q_W = ATT_HEADS * ATT_HEAD_DIM
ATT_KV_W = ATT_KV_HEADS * ATT_HEAD_DIM
M_QK_W = MLSTM_HEADS * MLSTM_QK_DIM
M_V_W = MLSTM_HEADS * MLSTM_V_DIM
M_GATE_W = 4 * MLSTM_HEADS

OFF_AQ = 0
OFF_AK = OFF_AQ + ATT_Q_W
OFF_AV = OFF_AK + ATT_KV_W
OFF_MQ = OFF_AV + ATT_KV_W
OFF_MV = OFF_MQ + 2 * M_QK_W
OFF_MO = OFF_MV + M_V_W
OFF_MG = OFF_MO + M_V_W
OFF_BG = OFF_MG + M_GATE_W
IN_WIDTH = OFF_BG + 2 * D_MODEL

LANES = 128
SUBLANES = 8
VMEM_LIMIT = 56 * 1024 * 1024

SCAN_CHUNK = 128
HALO = SUBLANES

BF16 = jnp.bfloat16
F32 = jnp.float32


def _rms(x, g):
    return x * lax.rsqrt(jnp.mean(x * x, axis=-1, keepdims=True) + RMS_EPS) * g


def _sigmoid(x):
    return 1.0 / (1.0 + jnp.exp(-x))


def _dot(a, b):
    return jnp.dot(a, b, preferred_element_type=F32)


def _dot_nt(a, b):
    return lax.dot_general(a, b, (((1,), (1,)), ((), ())), preferred_element_type=F32)


def _ffn_kernel(n_hid, with_ple, *refs):
    if with_ple:
        (h_ref, gpre_ref, w1a_ref, w1b_ref, w2_ref, gpost_ref, p_ref, gple_pre_ref, wpg_ref, wpp_ref,
         gple_post_ref, o_ref, xn_ref, acc_ref) = refs
    else:
        h_ref, gpre_ref, w1a_ref, w1b_ref, w2_ref, gpost_ref, o_ref, xn_ref, acc_ref = refs
    j = pl.program_id(1)

    @pl.when(j == 0)
    def _():
        xn_ref[...] = _rms(h_ref[...], gpre_ref[...]).astype(BF16)

    xn = xn_ref[...]
    z1 = _dot(xn, w1a_ref[...])
    z2 = _dot(xn, w1b_ref[...])
    act = (z1 * _sigmoid(z1) * z2).astype(BF16)
    part = _dot(act, w2_ref[...])

    @pl.when(j == 0)
    def _():
        acc_ref[...] = part

    @pl.when(j > 0)
    def _():
        acc_ref[...] += part

    @pl.when(j == n_hid - 1)
    def _():
        h = h_ref[...] + MACARON * _rms(acc_ref[...], gpost_ref[...])
        if with_ple:
            gate = _sigmoid(_dot(_rms(h, gple_pre_ref[...]).astype(BF16), wpg_ref[...]))
            proj = _dot(p_ref[...].astype(BF16), wpp_ref[...])
            h = h + _rms(proj * gate, gple_post_ref[...])
        o_ref[...] = h


def _ffn(h, g_pre, w1a, w1b, w2, g_post, ple=None, *, tm=512, th=1408, name="ffn"):
    t, d = h.shape
    n_hid = FFN_HIDDEN // th
    row = lambda i, j: (i, 0)
    const = lambda i, j: (0, 0)
    in_specs = [
        pl.BlockSpec((tm, d), row),
        pl.BlockSpec((1, d), const),
        pl.BlockSpec((d, th), lambda i, j: (0, j)),
        pl.BlockSpec((d, th), lambda i, j: (0, j)),
        pl.BlockSpec((th, d), lambda i, j: (j, 0)),
        pl.BlockSpec((1, d), const),
    ]
    args = [h, g_pre, w1a, w1b, w2, g_post]
    if ple is not None:
        p, g_ple_pre, w_gate, w_proj, g_ple_post = ple
        in_specs += [
            pl.BlockSpec((tm, PLE_DIM), row),
            pl.BlockSpec((1, d), const),
            pl.BlockSpec((d, d), const),
            pl.BlockSpec((PLE_DIM, d), const),
            pl.BlockSpec((1, d), const),
        ]
        args += [p, g_ple_pre, w_gate, w_proj, g_ple_post]
    return pl.pallas_call(
        functools.partial(_ffn_kernel, n_hid, ple is not None),
        grid=(t // tm, n_hid),
        in_specs=in_specs,
        out_specs=pl.BlockSpec((tm, d), row),
        out_shape=jax.ShapeDtypeStruct((t, d), F32),
        scratch_shapes=[pltpu.VMEM((tm, d), BF16), pltpu.VMEM((tm, d), F32)],
        compiler_params=pltpu.CompilerParams(
            dimension_semantics=("arbitrary", "arbitrary"), vmem_limit_bytes=VMEM_LIMIT),
        name=name,
    )(*args)


def _inproj_kernel(h_ref, g_ref, w_ref, bg_ref, rc_ref, rs1_ref, rs2_ref,
                   q_ref, k_ref, v_ref, mqk_ref, mv_ref, gate_ref):
    u = _rms(h_ref[...], g_ref[...]).astype(BF16)
    rc, rs1, rs2 = rc_ref[...], rs1_ref[...], rs2_ref[...]

    def rope(x):
        half = ROPE_DIM // 2
        return x * rc + pltpu.roll(x, LANES - half, 1) * rs1 + pltpu.roll(x, half, 1) * rs2

    zq = _dot(u, w_ref[:, OFF_AQ:OFF_AK])
    for c in range(ATT_Q_W // LANES):
        q_ref[:, c * LANES:(c + 1) * LANES] = (
            rope(zq[:, c * LANES:(c + 1) * LANES]) * (ATT_HEAD_DIM ** -0.5)).astype(BF16)
    zkv = _dot(u, w_ref[:, OFF_AK:OFF_MQ])
    k_ref[...] = rope(zkv[:, :ATT_KV_W]).astype(BF16)
    v_ref[...] = zkv[:, ATT_KV_W:].astype(BF16)
    mqk_ref[...] = _dot(u, w_ref[:, OFF_MQ:OFF_MV])
    mv_ref[...] = _dot(u, w_ref[:, OFF_MV:OFF_MO]).astype(BF16)
    g = _dot(u, w_ref[:, OFF_MO:OFF_MO + LANES]) + bg_ref[...]
    lane = lax.broadcasted_iota(jnp.int32, g.shape, 1)
    log_sig = jnp.minimum(g, 0.0) - jnp.log1p(jnp.exp(-jnp.abs(g)))
    gate_ref[...] = jnp.where((lane // MLSTM_HEADS) % 2 == 1, log_sig, g)


def _inproj(h3, g_pre, w_a, b_gate, rope_tabs, *, tm=512):
    b, s, d = h3.shape
    wa = w_a.shape[1]
    tok = lambda i, bb: (bb, i, 0)
    const = lambda i, bb: (0, 0)
    tab = lambda i, bb: (i, 0)
    out_w = (ATT_Q_W, ATT_KV_W, ATT_KV_W, 2 * M_QK_W, M_V_W, LANES)
    out_dt = (BF16, BF16, BF16, F32, BF16, F32)
    return pl.pallas_call(
        _inproj_kernel,
        grid=(s // tm, b),
        in_specs=[
            pl.BlockSpec((None, tm, d), tok),
            pl.BlockSpec((1, d), const),
            pl.BlockSpec((d, wa), const),
            pl.BlockSpec((1, LANES), const),
            pl.BlockSpec((tm, LANES), tab),
            pl.BlockSpec((tm, LANES), tab),
            pl.BlockSpec((tm, LANES), tab),
        ],
        out_specs=[pl.BlockSpec((None, tm, w), tok) for w in out_w],
        out_shape=[jax.ShapeDtypeStruct((b, s, w), dt) for w, dt in zip(out_w, out_dt)],
        compiler_params=pltpu.CompilerParams(
            dimension_semantics=("arbitrary", "arbitrary"), vmem_limit_bytes=VMEM_LIMIT),
        name="inproj",
    )(h3, g_pre, w_a, b_gate, *rope_tabs)


def _attn_kernel(n_blk, sink_ref, q_ref, kp_ref, kc_ref, kn_ref, vp_ref, vc_ref, vn_ref, o_ref):
    i = pl.program_id(1)
    blk = ATT_BLOCK
    qi = lax.broadcasted_iota(jnp.int32, (blk, 3 * blk), 0)
    kc = lax.broadcasted_iota(jnp.int32, (blk, 3 * blk), 1)
    rel = kc - blk - qi
    mask = (jnp.abs(rel) <= ATT_WINDOW) & ((kc >= blk) | (i > 0)) & ((kc < 2 * blk) | (i < n_blk - 1))
    q = q_ref[...]
    kcat = jnp.concatenate([kp_ref[...], kc_ref[...], kn_ref[...]], axis=0)
    vcat = jnp.concatenate([vp_ref[...], vc_ref[...], vn_ref[...]], axis=0)
    group = ATT_HEADS // ATT_KV_HEADS
    dh = ATT_HEAD_DIM
    outs = []
    for h in range(ATT_HEADS):
        hk = h // group
        kh = kcat[:, hk * dh:(hk + 1) * dh]
        vh = vcat[:, hk * dh:(hk + 1) * dh]
        s = _dot_nt(q[:, h * dh:(h + 1) * dh], kh)
        s = jnp.where(mask, s, -jnp.inf)
        sink = sink_ref[h]
        mx = jnp.maximum(jnp.max(s, axis=-1, keepdims=True), sink)
        p = jnp.exp(s - mx)
        denom = jnp.sum(p, axis=-1, keepdims=True) + jnp.exp(sink - mx)
        outs.append(_dot(p.astype(BF16), vh) / denom)
    o_ref[...] = jnp.concatenate(outs, axis=-1).astype(BF16)


def _attention(q, k, v, sink):
    b, s, _ = q.shape
    n_blk = s // ATT_BLOCK
    cur = lambda bb, i: (bb, i, 0)
    prev = lambda bb, i: (bb, jnp.maximum(i - 1, 0), 0)
    nxt = lambda bb, i: (bb, jnp.minimum(i + 1, n_blk - 1), 0)
    kv_spec = lambda im: pl.BlockSpec((None, ATT_BLOCK, ATT_KV_W), im)
    return pl.pallas_call(
        functools.partial(_attn_kernel, n_blk),
        grid=(b, n_blk),
        in_specs=[
            pl.BlockSpec(memory_space=pltpu.SMEM),
            pl.BlockSpec((None, ATT_BLOCK, ATT_Q_W), cur),
            kv_spec(prev), kv_spec(cur), kv_spec(nxt),
            kv_spec(prev), kv_spec(cur), kv_spec(nxt),
        ],
        out_specs=pl.BlockSpec((None, ATT_BLOCK, ATT_Q_W), cur),
        out_shape=jax.ShapeDtypeStruct((b, s, ATT_Q_W), BF16),
        compiler_params=pltpu.CompilerParams(
            dimension_semantics=("arbitrary", "arbitrary"), vmem_limit_bytes=VMEM_LIMIT),
        name="attn",
    )(sink, q, k, k, k, v, v, v)


def _split3(x):
    hi = x.astype(BF16)
    r1 = x - hi.astype(F32)
    mid = r1.astype(BF16)
    lo = (r1 - mid.astype(F32)).astype(BF16)
    return hi, mid, lo


def _mlstm_kernel(n_chunk, cw_ref, cb_ref,
                  xf_ref, xfp_ref, xfn_ref, vf_ref, gf_ref,
                  xb_ref, xbp_ref, xbn_ref, vb_ref, gb_ref,
                  of_ref, ob_ref, xe_ref, c_ref, n_ref, m_ref):
    c = pl.program_id(1)
    L = SCAN_CHUNK
    H = MLSTM_HEADS
    dk = MLSTM_QK_DIM
    dv = MLSTM_V_DIM

    @pl.when(c == 0)
    def _():
        c_ref[...] = jnp.zeros_like(c_ref)
        n_ref[...] = jnp.zeros_like(n_ref)
        m_ref[...] = jnp.zeros_like(m_ref)

    row = lax.broadcasted_iota(jnp.int32, (L, L), 0)
    col = lax.broadcasted_iota(jnp.int32, (L, L), 1)
    lower = col <= row
    upper = col >= row
    tri_lo = jnp.where(lower, 1.0, 0.0).astype(BF16)
    tri_up = jnp.where(upper, 1.0, 0.0).astype(BF16)
    cw = cw_ref[...]
    cbias = cb_ref[...]

    dirs = (
        (0, c, xf_ref, xfp_ref, xfn_ref, vf_ref, gf_ref, of_ref),
        (1, n_chunk - 1 - c, xb_ref, xbp_ref, xbn_ref, vb_ref, gb_ref, ob_ref),
    )
    for d, idx, x_ref, xp_ref, xn_ref, v_ref, g_ref, o_ref in dirs:
        xe = xe_ref.at[d]
        xe[0:HALO, :] = jnp.where(idx > 0, xp_ref[...], 0.0)
        xe[HALO:HALO + L, :] = x_ref[...]
        xe[HALO + L:, :] = jnp.where(idx < n_chunk - 1, xn_ref[...], 0.0)
        y = cbias
        for j in range(MLSTM_CONV):
            off = HALO - MLSTM_CONV // 2 + j
            y = y + xe[off:off + L, :] * cw[j:j + 1, :]
        qk = y * _sigmoid(y)

        g = g_ref[...]
        gt = g.T
        g3 = _split3(g)
        gt3 = _split3(gt)
        if d == 0:
            b_col = sum(_dot(tri_lo, p) for p in g3)
            b_row = sum(_dot(p, tri_up) for p in gt3)
            causal = lower
        else:
            b_col = sum(_dot(tri_up, p) for p in g3)
            b_row = sum(_dot(p, tri_lo) for p in gt3)
            causal = upper
        last = L - 1 if d == 0 else 0

        for h in range(H):
            r = d * H + h
            ci = 2 * d * H + h
            cf = ci + H
            q = qk[:, h * dk:(h + 1) * dk]
            k = qk[:, M_QK_W + h * dk:M_QK_W + (h + 1) * dk] * (dk ** -0.5)
            v = v_ref[:, h * dv:(h + 1) * dv]
            m_prev = m_ref[r:r + 1, 0:1]
            n_prev = n_ref[r:r + 1, :]
            c_prev = c_ref[r]

            b_c = b_col[:, cf:cf + 1]
            b_r = b_row[cf:cf + 1, :]
            li_c = g[:, ci:ci + 1]
            li_r = gt[ci:ci + 1, :]
            d_log = jnp.where(causal, b_c - b_r + li_r, -jnp.inf)
            inter = b_c + m_prev
            m_row = jnp.maximum(inter, jnp.max(d_log, axis=-1, keepdims=True))
            w_intra = jnp.exp(d_log - m_row)
            w_inter = jnp.exp(inter - m_row)
            qb = q.astype(BF16)
            sc = _dot_nt(qb, k.astype(BF16)) * w_intra
            num = _dot(sc.astype(BF16), v) + w_inter * _dot(qb, c_prev.astype(BF16))
            den = jnp.sum(sc, axis=-1, keepdims=True) + w_inter * jnp.sum(q * n_prev, axis=-1, keepdims=True)
            o_ref[:, h * dv:(h + 1) * dv] = num / jnp.maximum(jnp.abs(den), jnp.exp(-m_row))

            b_last = b_col[last:last + 1, cf:cf + 1]
            m_new = jnp.maximum(b_last + m_prev, jnp.max(b_last - b_r + li_r, axis=-1, keepdims=True))
            w_k = jnp.exp(b_last - b_c + li_c - m_new)
            decay = jnp.exp(b_last + m_prev - m_new)
            kw = k * w_k
            c_ref[r] = decay * c_prev + _dot(kw.T.astype(BF16), v)
            n_ref[r:r + 1, :] = decay * n_prev + jnp.sum(kw, axis=0, keepdims=True)
            m_ref[r:r + 1, :] = jnp.broadcast_to(m_new, (1, LANES))


def _mlstm(mqk, mv, gates, conv_w, conv_b):
    b, s, _ = mqk.shape
    L = SCAN_CHUNK
    n_chunk = s // L
    hpc = L // HALO
    n_halo = s // HALO
    fwd = lambda bb, c: (bb, c, 0)
    bwd = lambda bb, c: (bb, n_chunk - 1 - c, 0)
    fwd_p = lambda bb, c: (bb, jnp.maximum(c * hpc - 1, 0), 0)
    fwd_n = lambda bb, c: (bb, jnp.minimum((c + 1) * hpc, n_halo - 1), 0)
    bwd_p = lambda bb, c: (bb, jnp.maximum((n_chunk - 1 - c) * hpc - 1, 0), 0)
    bwd_n = lambda bb, c: (bb, jnp.minimum((n_chunk - c) * hpc, n_halo - 1), 0)
    const = lambda bb, c: (0, 0)
    qk_w = 2 * M_QK_W

    def dir_specs(main, p, n):
        return [
            pl.BlockSpec((None, L, qk_w), main),
            pl.BlockSpec((None, HALO, qk_w), p),
            pl.BlockSpec((None, HALO, qk_w), n),
            pl.BlockSpec((None, L, M_V_W), main),
            pl.BlockSpec((None, L, LANES), main),
        ]

    n_state = 2 * MLSTM_HEADS
    return pl.pallas_call(
        functools.partial(_mlstm_kernel, n_chunk),
        grid=(b, n_chunk),
        in_specs=[pl.BlockSpec((MLSTM_CONV, qk_w), const), pl.BlockSpec((1, qk_w), const)]
        + dir_specs(fwd, fwd_p, fwd_n) + dir_specs(bwd, bwd_p, bwd_n),
        out_specs=[pl.BlockSpec((None, L, M_V_W), fwd), pl.BlockSpec((None, L, M_V_W), bwd)],
        out_shape=[jax.ShapeDtypeStruct((b, s, M_V_W), F32)] * 2,
        scratch_shapes=[
            pltpu.VMEM((2, L + 2 * HALO, qk_w), F32),
            pltpu.VMEM((n_state, MLSTM_QK_DIM, MLSTM_V_DIM), F32),
            pltpu.VMEM((n_state, LANES), F32),
            pltpu.VMEM((n_state, LANES), F32),
        ],
        compiler_params=pltpu.CompilerParams(
            dimension_semantics=("arbitrary", "arbitrary"), vmem_limit_bytes=VMEM_LIMIT),
        name="mlstm",
    )(conv_w, conv_b, mqk, mqk, mqk, mv, gates, mqk, mqk, mqk, mv, gates)


def _merge_kernel(h_ref, oa_ref, hf_ref, hb_ref, gpre_ref, wmo_ref, wbg_ref, wba_ref, wbm_ref, wout_ref,
                  gm_ref, gpost_ref, o_ref):
    h = h_ref[...]
    u = _rms(h, gpre_ref[...]).astype(BF16)
    hm = hf_ref[...] + hb_ref[...]
    dv = MLSTM_V_DIM
    hm = jnp.concatenate(
        [hm[:, i * dv:(i + 1) * dv]
         * lax.rsqrt(jnp.mean(hm[:, i * dv:(i + 1) * dv] ** 2, axis=-1, keepdims=True) + RMS_EPS)
         for i in range(MLSTM_HEADS)], axis=-1)
    hm = hm * gm_ref[...] * _sigmoid(_dot(u, wmo_ref[...]))
    ym = _dot(hm.astype(BF16), wbm_ref[...])
    ya = _dot(oa_ref[...], wba_ref[...])
    merged = (_sigmoid(_dot(u, wbg_ref[:, :D_MODEL])) * ya
              + _sigmoid(_dot(u, wbg_ref[:, D_MODEL:])) * ym)
    mix = _dot(merged.astype(BF16), wout_ref[...])
    o_ref[...] = h + _rms(mix, gpost_ref[...])


def _merge(h, oa, hf, hb, g_pre, w_mo, w_bg, w_ba, w_bm, w_out, g_m, g_post, *, tm=256):
    t, d = h.shape
    row = lambda i: (i, 0)
    const = lambda i: (0, 0)
    full = lambda a: pl.BlockSpec(a.shape, const)
    return pl.pallas_call(
        _merge_kernel,
        grid=(t // tm,),
        in_specs=[
            pl.BlockSpec((tm, d), row), pl.BlockSpec((tm, ATT_Q_W), row),
            pl.BlockSpec((tm, M_V_W), row), pl.BlockSpec((tm, M_V_W), row),
            full(g_pre), full(w_mo), full(w_bg), full(w_ba), full(w_bm), full(w_out), full(g_m), full(g_post),
        ],
        out_specs=pl.BlockSpec((tm, d), row),
        out_shape=jax.ShapeDtypeStruct((t, d), F32),
        compiler_params=pltpu.CompilerParams(
            dimension_semantics=("arbitrary",), vmem_limit_bytes=VMEM_LIMIT),
        name="merge",
    )(h, oa, hf, hb, g_pre, w_mo, w_bg, w_ba, w_bm, w_out, g_m, g_post)


def _rope_tables(seq):
    pos = jnp.arange(seq, dtype=F32)
    inv = ROPE_THETA ** (-jnp.arange(0, ROPE_DIM, 2, dtype=F32) / ROPE_DIM)
    ang = pos[:, None] * inv[None, :]
    cos, sin = jnp.cos(ang), jnp.sin(ang)
    half = ROPE_DIM // 2
    dpos = jnp.arange(LANES) % ATT_HEAD_DIM
    sel = dpos % half
    rc = jnp.where(dpos < ROPE_DIM, cos[:, sel], 1.0)
    rs1 = jnp.where(dpos < half, -sin[:, sel], 0.0)
    rs2 = jnp.where((dpos >= half) & (dpos < ROPE_DIM), sin[:, sel], 0.0)
    return rc, rs1, rs2


def kernel(x, p, ffn1_norm_pre, ffn1_w1, ffn1_w2, ffn1_norm_post, mix_norm_pre, w_in, b_gates, conv_w, conv_b,
           attn_sink, mlstm_norm, w_branch_attn, w_branch_mlstm, w_out, mix_norm_post, ffn2_norm_pre, ffn2_w1,
           ffn2_w2, ffn2_norm_post, ple_norm_pre, w_ple_gate, w_ple_proj, ple_norm_post):
    bsz, seq, d = x.shape
    depth = p.shape[0]
    h = x.reshape(bsz * seq, d)
    rope_tabs = _rope_tables(seq)
    bf = lambda a: a.astype(BF16)
    for i in range(depth):
        h = _ffn(h, ffn1_norm_pre[i][None], bf(ffn1_w1[i][:, :FFN_HIDDEN]), bf(ffn1_w1[i][:, FFN_HIDDEN:]),
                 bf(ffn1_w2[i]), ffn1_norm_post[i][None], name="ffn1")
        wi = w_in[i]
        w_a = bf(jnp.concatenate(
            [wi[:, :OFF_MO], jnp.pad(wi[:, OFF_MG:OFF_BG], ((0, 0), (0, LANES - M_GATE_W)))], axis=1))
        b_gate = jnp.pad(b_gates[i], (0, LANES - M_GATE_W))[None]
        q, k, v, mqk, mv, gates = _inproj(h.reshape(bsz, seq, d), mix_norm_pre[i][None], w_a, b_gate, rope_tabs)
        oa = _attention(q, k, v, attn_sink[i])
        hf, hb = _mlstm(mqk, mv, gates, conv_w[i], conv_b[i][None])
        h = _merge(h, oa.reshape(bsz * seq, ATT_Q_W), hf.reshape(bsz * seq, M_V_W), hb.reshape(bsz * seq, M_V_W),
                   mix_norm_pre[i][None], bf(wi[:, OFF_MO:OFF_MG]), bf(wi[:, OFF_BG:]), bf(w_branch_attn[i]),
                   bf(w_branch_mlstm[i]), bf(w_out[i]), mlstm_norm[i][None], mix_norm_post[i][None])
        h = _ffn(h, ffn2_norm_pre[i][None], bf(ffn2_w1[i][:, :FFN_HIDDEN]), bf(ffn2_w1[i][:, FFN_HIDDEN:]),
                 bf(ffn2_w2[i]), ffn2_norm_post[i][None],
                 ple=(p[i].reshape(bsz * seq, PLE_DIM), ple_norm_pre[i][None], bf(w_ple_gate[i]),
                      bf(w_ple_proj[i]), ple_norm_post[i][None]), name="ffn2_ple")
    return h.reshape(bsz, seq, d)
```

```python
import functools

import jax
import jax.numpy as jnp
from jax import lax
from jax.experimental import pallas as pl
from jax.experimental.pallas import tpu as pltpu

D_MODEL = 1024
PLE_DIM = 256
RMS_EPS = 1e-6
MACARON = 0.5
FFN_HIDDEN = 2816
ATT_HEADS = 8
ATT_KV_HEADS = 2
ATT_HEAD_DIM = 64
ATT_WINDOW = 128
ATT_BLOCK = 128
ROPE_THETA = 500000.0
ROPE_DIM = ATT_HEAD_DIM // 4
MLSTM_HEADS = 4
MLSTM_QK_DIM = 128
MLSTM_V_DIM = 256
MLSTM_CONV = 5
ATT_Q_W = ATT_HEADS * ATT_HEAD_DIM
ATT_KV_W = ATT_KV_HEADS * ATT_HEAD_DIM
M_QK_W = MLSTM_HEADS * MLSTM_QK_DIM
M_V_W = MLSTM_HEADS * MLSTM_V_DIM
M_GATE_W = 4 * MLSTM_HEADS

OFF_AQ = 0
OFF_AK = OFF_AQ + ATT_Q_W
OFF_AV = OFF_AK + ATT_KV_W
OFF_MQ = OFF_AV + ATT_KV_W
OFF_MV = OFF_MQ + 2 * M_QK_W
OFF_MO = OFF_MV + M_V_W
OFF_MG = OFF_MO + M_V_W
OFF_BG = OFF_MG + M_GATE_W
IN_WIDTH = OFF_BG + 2 * D_MODEL

LANES = 128
SUBLANES = 8
BF16_ROWS = 16
VMEM_LIMIT = 56 * 1024 * 1024

SCAN_CHUNK = 256
N_STATE = 2 * MLSTM_HEADS
V_AUG = MLSTM_V_DIM + LANES
CONV_HALO = BF16_ROWS

BF16 = jnp.bfloat16
F32 = jnp.float32


def _rms(x, g):
    return x * lax.rsqrt(jnp.mean(x * x, axis=-1, keepdims=True) + RMS_EPS) * g


def _sigmoid(x):
    return 1.0 / (1.0 + jnp.exp(-x))


def _dot(a, b):
    return jnp.dot(a, b, preferred_element_type=F32)


def _dot_nt(a, b):
    return lax.dot_general(a, b, (((1,), (1,)), ((), ())), preferred_element_type=F32)


def _ones_column_block(rows, width):
    return jnp.where(lax.broadcasted_iota(jnp.int32, (rows, width), 1) == 0, 1.0, 0.0).astype(BF16)


def _ffn_kernel(n_hid, with_ple, *refs):
    if with_ple:
        (h_ref, gpre_ref, w1a_ref, w1b_ref, w2_ref, gpost_ref, p_ref, gple_pre_ref, wpg_ref, wpp_ref,
         gple_post_ref, o_ref, xn_ref, acc_ref) = refs
    else:
        h_ref, gpre_ref, w1a_ref, w1b_ref, w2_ref, gpost_ref, o_ref, xn_ref, acc_ref = refs
    j = pl.program_id(1)

    @pl.when(j == 0)
    def _():
        xn_ref[...] = _rms(h_ref[...], gpre_ref[...]).astype(BF16)

    xn = xn_ref[...]
    z1 = _dot(xn, w1a_ref[...])
    z2 = _dot(xn, w1b_ref[...])
    act = (z1 * _sigmoid(z1) * z2).astype(BF16)
    part = _dot(act, w2_ref[...])

    @pl.when(j == 0)
    def _():
        acc_ref[...] = part

    @pl.when(j > 0)
    def _():
        acc_ref[...] += part

    @pl.when(j == n_hid - 1)
    def _():
        h = h_ref[...] + MACARON * _rms(acc_ref[...], gpost_ref[...])
        if with_ple:
            gate = _sigmoid(_dot(_rms(h, gple_pre_ref[...]).astype(BF16), wpg_ref[...]))
            proj = _dot(p_ref[...].astype(BF16), wpp_ref[...])
            h = h + _rms(proj * gate, gple_post_ref[...])
        o_ref[...] = h


def _ffn(h, g_pre, w1a, w1b, w2, g_post, ple=None, *, tm=512, th=1408, name="ffn"):
    t, d = h.shape
    n_hid = FFN_HIDDEN // th
    row = lambda i, j: (i, 0)
    const = lambda i, j: (0, 0)
    in_specs = [
        pl.BlockSpec((tm, d), row),
        pl.BlockSpec((1, d), const),
        pl.BlockSpec((d, th), lambda i, j: (0, j)),
        pl.BlockSpec((d, th), lambda i, j: (0, j)),
        pl.BlockSpec((th, d), lambda i, j: (j, 0)),
        pl.BlockSpec((1, d), const),
    ]
    args = [h, g_pre, w1a, w1b, w2, g_post]
    if ple is not None:
        p, g_ple_pre, w_gate, w_proj, g_ple_post = ple
        in_specs += [
            pl.BlockSpec((tm, PLE_DIM), row),
            pl.BlockSpec((1, d), const),
            pl.BlockSpec((d, d), const),
            pl.BlockSpec((PLE_DIM, d), const),
            pl.BlockSpec((1, d), const),
        ]
        args += [p, g_ple_pre, w_gate, w_proj, g_ple_post]
    return pl.pallas_call(
        functools.partial(_ffn_kernel, n_hid, ple is not None),
        grid=(t // tm, n_hid),
        in_specs=in_specs,
        out_specs=pl.BlockSpec((tm, d), row),
        out_shape=jax.ShapeDtypeStruct((t, d), F32),
        scratch_shapes=[pltpu.VMEM((tm, d), BF16), pltpu.VMEM((tm, d), F32)],
        compiler_params=pltpu.CompilerParams(
            dimension_semantics=("arbitrary", "arbitrary"), vmem_limit_bytes=VMEM_LIMIT),
        name=name,
    )(*args)


WA_Q = 0
WA_KV = WA_Q + ATT_Q_W
WA_MQK = WA_KV + 2 * ATT_KV_W
WA_MV = WA_MQK + 2 * M_QK_W
WA_G = WA_MV + M_V_W
WA_END = WA_G + LANES


def _lane_scan(x, op, ident, reverse):
    width = x.shape[1]
    lane = lax.broadcasted_iota(jnp.int32, x.shape, 1)
    s = 1
    while s < width:
        if reverse:
            shifted = jnp.where(lane < width - s, pltpu.roll(x, width - s, 1), ident)
        else:
            shifted = jnp.where(lane >= s, pltpu.roll(x, s, 1), ident)
        x = op(x, shifted)
        s *= 2
    return x


def _inproj_kernel(n_tile, h_ref, hp_ref, hn_ref, g_ref, w_ref, bg_ref, rc_ref, rs1_ref, rs2_ref, cw_ref, cb_ref,
                   q_ref, k_ref, v_ref, mq_ref, mkt_ref, mv_ref, gcol_ref, grow_ref, ue_ref, ze_ref):
    i = pl.program_id(0)
    tm = h_ref.shape[0]
    g_pre = g_ref[...]
    halo = CONV_HALO
    ue_ref[0:halo, :] = jnp.where(i > 0, _rms(hp_ref[...], g_pre), 0.0).astype(BF16)
    ue_ref[halo:halo + tm, :] = _rms(h_ref[...], g_pre).astype(BF16)
    ue_ref[halo + tm:, :] = jnp.where(i < n_tile - 1, _rms(hn_ref[...], g_pre), 0.0).astype(BF16)
    u = ue_ref[halo:halo + tm, :]
    rc, rs1, rs2 = rc_ref[...], rs1_ref[...], rs2_ref[...]

    def rope(x):
        half = ROPE_DIM // 2
        return x * rc + pltpu.roll(x, LANES - half, 1) * rs1 + pltpu.roll(x, half, 1) * rs2

    zq = _dot(u, w_ref[:, WA_Q:WA_KV])
    for c in range(ATT_Q_W // LANES):
        q_ref[:, c * LANES:(c + 1) * LANES] = (
            rope(zq[:, c * LANES:(c + 1) * LANES]) * (ATT_HEAD_DIM ** -0.5)).astype(BF16)
    zkv = _dot(u, w_ref[:, WA_KV:WA_MQK])
    k_ref[...] = rope(zkv[:, :ATT_KV_W]).astype(BF16)
    v_ref[...] = zkv[:, ATT_KV_W:].astype(BF16)
    mv_ref[...] = _dot(u, w_ref[:, WA_MV:WA_G]).astype(BF16)

    ze_ref[...] = _dot(ue_ref[...], w_ref[:, WA_MQK:WA_MV])
    cw = cw_ref[...]
    y = cb_ref[...]
    for j in range(MLSTM_CONV):
        off = halo - MLSTM_CONV // 2 + j
        y = y + ze_ref[off:off + tm, :] * cw[j:j + 1, :]
    qk = y * _sigmoid(y)
    mq_ref[...] = qk[:, :M_QK_W].astype(BF16)
    mkt_ref[...] = (qk[:, M_QK_W:] * (MLSTM_QK_DIM ** -0.5)).T.astype(BF16)

    gt = (_dot(u, w_ref[:, WA_G:WA_END]) + bg_ref[...]).T
    li = gt[0:N_STATE]
    f_raw = gt[N_STATE:2 * N_STATE]
    lf = jnp.minimum(f_raw, 0.0) - jnp.log1p(jnp.exp(-jnp.abs(f_raw)))
    is_fwd = lax.broadcasted_iota(jnp.int32, (N_STATE, SCAN_CHUNK), 0) < MLSTM_HEADS
    a_parts, cm_parts, b_parts = [], [], []
    for c in range(tm // SCAN_CHUNK):
        sl = slice(c * SCAN_CHUNK, (c + 1) * SCAN_CHUNK)
        lfc = lf[:, sl]
        b = jnp.where(is_fwd, _lane_scan(lfc, jnp.add, 0.0, False), _lane_scan(lfc, jnp.add, 0.0, True))
        a = li[:, sl] - b
        cm = jnp.where(is_fwd, _lane_scan(a, jnp.maximum, -jnp.inf, False),
                       _lane_scan(a, jnp.maximum, -jnp.inf, True))
        a_parts.append(a)
        cm_parts.append(cm)
        b_parts.append(b)
    cat = lambda parts: jnp.concatenate(parts, axis=1)
    grow_ref[...] = cat(a_parts)
    rows = jnp.concatenate(
        [cat(cm_parts), cat(b_parts), jnp.zeros((LANES - 2 * N_STATE, tm), F32)], axis=0)
    gcol_ref[...] = rows.T


def _inproj(h3, g_pre, w_a, b_gate, rope_tabs, conv_w, conv_b, *, tm=512):
    b, s, d = h3.shape
    n_tile = s // tm
    hpt = tm // CONV_HALO
    n_halo = s // CONV_HALO
    tok = lambda i, bb: (bb, i, 0)
    tok_p = lambda i, bb: (bb, jnp.maximum(i * hpt - 1, 0), 0)
    tok_n = lambda i, bb: (bb, jnp.minimum((i + 1) * hpt, n_halo - 1), 0)
    const = lambda i, bb: (0, 0)
    tab = lambda i, bb: (i, 0)
    qk_w = 2 * M_QK_W
    tok_spec = lambda w: pl.BlockSpec((None, tm, w), tok)
    tr_spec = lambda r: pl.BlockSpec((None, r, tm), lambda i, bb: (bb, 0, i))
    return pl.pallas_call(
        functools.partial(_inproj_kernel, n_tile),
        grid=(n_tile, b),
        in_specs=[
            tok_spec(d),
            pl.BlockSpec((None, CONV_HALO, d), tok_p),
            pl.BlockSpec((None, CONV_HALO, d), tok_n),
            pl.BlockSpec((1, d), const),
            pl.BlockSpec((d, WA_END), const),
            pl.BlockSpec((1, LANES), const),
            pl.BlockSpec((tm, LANES), tab),
            pl.BlockSpec((tm, LANES), tab),
            pl.BlockSpec((tm, LANES), tab),
            pl.BlockSpec((MLSTM_CONV, qk_w), const),
            pl.BlockSpec((1, qk_w), const),
        ],
        out_specs=[tok_spec(ATT_Q_W), tok_spec(ATT_KV_W), tok_spec(ATT_KV_W), tok_spec(M_QK_W), tr_spec(M_QK_W),
                   tok_spec(M_V_W), tok_spec(LANES), tr_spec(N_STATE)],
        out_shape=[
            jax.ShapeDtypeStruct((b, s, ATT_Q_W), BF16),
            jax.ShapeDtypeStruct((b, s, ATT_KV_W), BF16),
            jax.ShapeDtypeStruct((b, s, ATT_KV_W), BF16),
            jax.ShapeDtypeStruct((b, s, M_QK_W), BF16),
            jax.ShapeDtypeStruct((b, M_QK_W, s), BF16),
            jax.ShapeDtypeStruct((b, s, M_V_W), BF16),
            jax.ShapeDtypeStruct((b, s, LANES), F32),
            jax.ShapeDtypeStruct((b, N_STATE, s), F32),
        ],
        scratch_shapes=[pltpu.VMEM((tm + 2 * CONV_HALO, d), BF16), pltpu.VMEM((tm + 2 * CONV_HALO, qk_w), F32)],
        compiler_params=pltpu.CompilerParams(
            dimension_semantics=("arbitrary", "arbitrary"), vmem_limit_bytes=VMEM_LIMIT),
        name="inproj",
    )(h3, h3, h3, g_pre, w_a, b_gate, *rope_tabs, conv_w, conv_b)


def _attn_kernel(seq, sink_ref, q_ref, kp_ref, kc_ref, kn_ref, vp_ref, vc_ref, vn_ref, o_ref):
    i = pl.program_id(1)
    blk = ATT_BLOCK
    tq = q_ref.shape[0]
    dh = ATT_HEAD_DIM
    group = ATT_HEADS // ATT_KV_HEADS
    rows = group * blk
    qi = lax.broadcasted_iota(jnp.int32, (rows, 3 * blk), 0) % blk
    kc = lax.broadcasted_iota(jnp.int32, (rows, 3 * blk), 1)
    band = jnp.abs(kc - blk - qi) <= ATT_WINDOW
    head_of_row = lax.broadcasted_iota(jnp.int32, (rows, 1), 0) // blk
    kcat = jnp.concatenate([kp_ref[...], kc_ref[...], kn_ref[...]], axis=0)
    vcat = jnp.concatenate([vp_ref[...], vc_ref[...], vn_ref[...]], axis=0)
    ones_blk = _ones_column_block(3 * blk, dh)
    for sb in range(tq // blk):
        kpos = i * tq + (sb - 1) * blk + kc
        mask = band & (kpos >= 0) & (kpos < seq)
        outs = []
        for g in range(ATT_KV_HEADS):
            qs = jnp.concatenate(
                [q_ref[sb * blk:(sb + 1) * blk, (g * group + hh) * dh:(g * group + hh + 1) * dh]
                 for hh in range(group)], axis=0)
            kk = kcat[sb * blk:(sb + 3) * blk, g * dh:(g + 1) * dh]
            vv = jnp.concatenate([vcat[sb * blk:(sb + 3) * blk, g * dh:(g + 1) * dh], ones_blk], axis=1)
            sink = jnp.zeros((rows, 1), F32)
            for hh in range(group):
                sink = jnp.where(head_of_row == hh, sink_ref[g * group + hh], sink)
            s = jnp.where(mask, _dot_nt(qs, kk), -jnp.inf)
            mx = jnp.maximum(jnp.max(s, axis=-1, keepdims=True), sink)
            p = jnp.exp(s - mx)
            pv = _dot(p.astype(BF16), vv)
            o = pv[:, :dh] / (pv[:, dh:dh + 1] + jnp.exp(sink - mx))
            outs += [o[hh * blk:(hh + 1) * blk] for hh in range(group)]
        o_ref[sb * blk:(sb + 1) * blk, :] = jnp.concatenate(outs, axis=-1).astype(BF16)


def _attention(q, k, v, sink, *, tq=512):
    b, s, _ = q.shape
    n_tile = s // tq
    bpt = tq // ATT_BLOCK
    n_blk = s // ATT_BLOCK
    cur = lambda bb, i: (bb, i, 0)
    prev = lambda bb, i: (bb, jnp.maximum(i * bpt - 1, 0), 0)
    nxt = lambda bb, i: (bb, jnp.minimum((i + 1) * bpt, n_blk - 1), 0)
    halo_spec = lambda im: pl.BlockSpec((None, ATT_BLOCK, ATT_KV_W), im)
    main_spec = pl.BlockSpec((None, tq, ATT_KV_W), cur)
    return pl.pallas_call(
        functools.partial(_attn_kernel, s),
        grid=(b, n_tile),
        in_specs=[
            pl.BlockSpec(memory_space=pltpu.SMEM),
            pl.BlockSpec((None, tq, ATT_Q_W), cur),
            halo_spec(prev), main_spec, halo_spec(nxt),
            halo_spec(prev), main_spec, halo_spec(nxt),
        ],
        out_specs=pl.BlockSpec((None, tq, ATT_Q_W), cur),
        out_shape=jax.ShapeDtypeStruct((b, s, ATT_Q_W), BF16),
        compiler_params=pltpu.CompilerParams(
            dimension_semantics=("arbitrary", "arbitrary"), vmem_limit_bytes=VMEM_LIMIT),
        name="attn",
    )(sink, q, k, k, k, v, v, v)


def _mlstm_kernel(n_chunk,
                  qf_ref, ktf_ref, vf_ref, gcf_ref, grf_ref,
                  qb_ref, ktb_ref, vb_ref, gcb_ref, grb_ref,
                  of_ref, ob_ref, c_ref, m_ref):
    c = pl.program_id(1)
    L = SCAN_CHUNK
    dk = MLSTM_QK_DIM
    dv = MLSTM_V_DIM

    @pl.when(c == 0)
    def _():
        c_ref[...] = jnp.zeros_like(c_ref)
        m_ref[...] = jnp.zeros_like(m_ref)

    row = lax.broadcasted_iota(jnp.int32, (L, L), 0)
    col = lax.broadcasted_iota(jnp.int32, (L, L), 1)
    ones_blk = _ones_column_block(L, LANES)
    dirs = (
        (0, qf_ref, ktf_ref, vf_ref, gcf_ref, grf_ref, of_ref),
        (1, qb_ref, ktb_ref, vb_ref, gcb_ref, grb_ref, ob_ref),
    )
    for d, q_ref, kt_ref, v_ref, gc_ref, gr_ref, o_ref in dirs:
        causal = (col <= row) if d == 0 else (col >= row)
        last = L - 1 if d == 0 else 0
        for h in range(MLSTM_HEADS):
            r = d * MLSTM_HEADS + h
            q = q_ref[:, h * dk:(h + 1) * dk]
            kt = kt_ref[h * dk:(h + 1) * dk, :]
            v_aug = jnp.concatenate([v_ref[:, h * dv:(h + 1) * dv], ones_blk], axis=1)
            a_r = gr_ref[r:r + 1, :]
            cm_c = gc_ref[:, r:r + 1]
            b_c = gc_ref[:, N_STATE + r:N_STATE + r + 1]
            m_prev = m_ref[r:r + 1, 0:1]
            c_prev = c_ref[r]

            m_c = jnp.maximum(cm_c, m_prev)
            w_intra = jnp.exp(jnp.where(causal, a_r - m_c, -jnp.inf))
            sc = (_dot(q, kt) * w_intra).astype(BF16)
            tot = _dot(sc, v_aug) + jnp.exp(m_prev - m_c) * _dot(q, c_prev.astype(BF16))
            den = jnp.maximum(jnp.abs(tot[:, dv:dv + 1]), jnp.exp(-b_c - m_c))
            o_ref[:, h * dv:(h + 1) * dv] = (tot[:, :dv] / den).astype(BF16)

            m_last = m_c[last:last + 1, :]
            w_k = jnp.exp(a_r - m_last)
            kw = (kt.astype(F32) * w_k).astype(BF16)
            c_ref[r] = jnp.exp(m_prev - m_last) * c_prev + _dot(kw, v_aug)
            m_ref[r:r + 1, :] = jnp.broadcast_to(b_c[last:last + 1, :] + m_last, (1, LANES))


def _mlstm(mq, mkt, mv, gcol, grow):
    b, s, _ = mq.shape
    L = SCAN_CHUNK
    n_chunk = s // L
    fwd = lambda bb, c: (bb, c, 0)
    bwd = lambda bb, c: (bb, n_chunk - 1 - c, 0)
    fwd_t = lambda bb, c: (bb, 0, c)
    bwd_t = lambda bb, c: (bb, 0, n_chunk - 1 - c)

    def dir_specs(tok, tr):
        return [
            pl.BlockSpec((None, L, M_QK_W), tok),
            pl.BlockSpec((None, M_QK_W, L), tr),
            pl.BlockSpec((None, L, M_V_W), tok),
            pl.BlockSpec((None, L, LANES), tok),
            pl.BlockSpec((None, N_STATE, L), tr),
        ]

    return pl.pallas_call(
        functools.partial(_mlstm_kernel, n_chunk),
        grid=(b, n_chunk),
        in_specs=dir_specs(fwd, fwd_t) + dir_specs(bwd, bwd_t),
        out_specs=[pl.BlockSpec((None, L, M_V_W), fwd), pl.BlockSpec((None, L, M_V_W), bwd)],
        out_shape=[jax.ShapeDtypeStruct((b, s, M_V_W), BF16)] * 2,
        scratch_shapes=[
            pltpu.VMEM((N_STATE, MLSTM_QK_DIM, V_AUG), F32),
            pltpu.VMEM((N_STATE, LANES), F32),
        ],
        compiler_params=pltpu.CompilerParams(
            dimension_semantics=("arbitrary", "arbitrary"), vmem_limit_bytes=VMEM_LIMIT),
        name="mlstm",
    )(mq, mkt, mv, gcol, grow, mq, mkt, mv, gcol, grow)


def _merge_kernel(h_ref, oa_ref, hf_ref, hb_ref, gpre_ref, wmo_ref, wbg_ref, wba_ref, wbm_ref, wout_ref,
                  gm_ref, gpost_ref, o_ref):
    h = h_ref[...]
    u = _rms(h, gpre_ref[...]).astype(BF16)
    hm = hf_ref[...].astype(F32) + hb_ref[...].astype(F32)
    dv = MLSTM_V_DIM
    hm = jnp.concatenate(
        [hm[:, i * dv:(i + 1) * dv]
         * lax.rsqrt(jnp.mean(hm[:, i * dv:(i + 1) * dv] ** 2, axis=-1, keepdims=True) + RMS_EPS)
         for i in range(MLSTM_HEADS)], axis=-1)
    hm = hm * gm_ref[...] * _sigmoid(_dot(u, wmo_ref[...]))
    ym = _dot(hm.astype(BF16), wbm_ref[...])
    ya = _dot(oa_ref[...], wba_ref[...])
    merged = (_sigmoid(_dot(u, wbg_ref[:, :D_MODEL])) * ya
              + _sigmoid(_dot(u, wbg_ref[:, D_MODEL:])) * ym)
    mix = _dot(merged.astype(BF16), wout_ref[...])
    o_ref[...] = h + _rms(mix, gpost_ref[...])


def _merge(h, oa, hf, hb, g_pre, w_mo, w_bg, w_ba, w_bm, w_out, g_m, g_post, *, tm=256):
    t, d = h.shape
    row = lambda i: (i, 0)
    const = lambda i: (0, 0)
    full = lambda a: pl.BlockSpec(a.shape, const)
    return pl.pallas_call(
        _merge_kernel,
        grid=(t // tm,),
        in_specs=[
            pl.BlockSpec((tm, d), row), pl.BlockSpec((tm, ATT_Q_W), row),
            pl.BlockSpec((tm, M_V_W), row), pl.BlockSpec((tm, M_V_W), row),
            full(g_pre), full(w_mo), full(w_bg), full(w_ba), full(w_bm), full(w_out), full(g_m), full(g_post),
        ],
        out_specs=pl.BlockSpec((tm, d), row),
        out_shape=jax.ShapeDtypeStruct((t, d), F32),
        compiler_params=pltpu.CompilerParams(
            dimension_semantics=("arbitrary",), vmem_limit_bytes=VMEM_LIMIT),
        name="merge",
    )(h, oa, hf, hb, g_pre, w_mo, w_bg, w_ba, w_bm, w_out, g_m, g_post)


def _rope_tables(seq):
    pos = jnp.arange(seq, dtype=F32)
    inv = ROPE_THETA ** (-jnp.arange(0, ROPE_DIM, 2, dtype=F32) / ROPE_DIM)
    ang = pos[:, None] * inv[None, :]
    cos, sin = jnp.cos(ang), jnp.sin(ang)
    half = ROPE_DIM // 2
    dpos = jnp.arange(LANES) % ATT_HEAD_DIM
    sel = dpos % half
    rc = jnp.where(dpos < ROPE_DIM, cos[:, sel], 1.0)
    rs1 = jnp.where(dpos < half, -sin[:, sel], 0.0)
    rs2 = jnp.where((dpos >= half) & (dpos < ROPE_DIM), sin[:, sel], 0.0)
    return rc, rs1, rs2


def _gate_columns(a):
    g = a.reshape(a.shape[:-1] + (4, MLSTM_HEADS))
    g = jnp.stack([g[..., 0, :], g[..., 2, :], g[..., 1, :], g[..., 3, :]], axis=-2)
    g = g.reshape(a.shape[:-1] + (M_GATE_W,))
    return jnp.pad(g, [(0, 0)] * (a.ndim - 1) + [(0, LANES - M_GATE_W)])


def kernel(x, p, ffn1_norm_pre, ffn1_w1, ffn1_w2, ffn1_norm_post, mix_norm_pre, w_in, b_gates, conv_w, conv_b,
           attn_sink, mlstm_norm, w_branch_attn, w_branch_mlstm, w_out, mix_norm_post, ffn2_norm_pre, ffn2_w1,
           ffn2_w2, ffn2_norm_post, ple_norm_pre, w_ple_gate, w_ple_proj, ple_norm_post):
    bsz, seq, d = x.shape
    depth = p.shape[0]
    h = x.reshape(bsz * seq, d)
    rope_tabs = _rope_tables(seq)
    bf = lambda a: a.astype(BF16)
    for i in range(depth):
        h = _ffn(h, ffn1_norm_pre[i][None], bf(ffn1_w1[i][:, :FFN_HIDDEN]), bf(ffn1_w1[i][:, FFN_HIDDEN:]),
                 bf(ffn1_w2[i]), ffn1_norm_post[i][None], name="ffn1")
        wi = w_in[i]
        w_a = bf(jnp.concatenate([wi[:, :OFF_MO], _gate_columns(wi[:, OFF_MG:OFF_BG])], axis=1))
        b_gate = _gate_columns(b_gates[i])[None]
        q, k, v, mq, mkt, mv, gcol, grow = _inproj(
            h.reshape(bsz, seq, d), mix_norm_pre[i][None], w_a, b_gate, rope_tabs, conv_w[i], conv_b[i][None])
        oa = _attention(q, k, v, attn_sink[i])
        hf, hb = _mlstm(mq, mkt, mv, gcol, grow)
        h = _merge(h, oa.reshape(bsz * seq, ATT_Q_W), hf.reshape(bsz * seq, M_V_W), hb.reshape(bsz * seq, M_V_W),
                   mix_norm_pre[i][None], bf(wi[:, OFF_MO:OFF_MG]), bf(wi[:, OFF_BG:]), bf(w_branch_attn[i]),
                   bf(w_branch_mlstm[i]), bf(w_out[i]), mlstm_norm[i][None], mix_norm_post[i][None])
        h = _ffn(h, ffn2_norm_pre[i][None], bf(ffn2_w1[i][:, :FFN_HIDDEN]), bf(ffn2_w1[i][:, FFN_HIDDEN:]),
                 bf(ffn2_w2[i]), ffn2_norm_post[i][None],
                 ple=(p[i].reshape(bsz * seq, PLE_DIM), ple_norm_pre[i][None], bf(w_ple_gate[i]),
                      bf(w_ple_proj[i]), ple_norm_post[i][None]), name="ffn2_ple")
    return h.reshape(bsz, seq, d)
```

```python
import functools

import jax
import jax.numpy as jnp
from jax import lax
from jax.experimental import pallas as pl
from jax.experimental.pallas import tpu as pltpu

D_MODEL = 1024
PLE_DIM = 256
RMS_EPS = 1e-6
MACARON = 0.5
FFN_HIDDEN = 2816
ATT_HEADS = 8
ATT_KV_HEADS = 2
ATT_HEAD_DIM = 64
ATT_WINDOW = 128
ATT_BLOCK = 128
ROPE_THETA = 500000.0
ROPE_DIM = ATT_HEAD_DIM // 4
MLSTM_HEADS = 4
MLSTM_QK_DIM = 128
MLSTM_V_DIM = 256
MLSTM_CONV = 5
ATT_Q_W = ATT_HEADS * ATT_HEAD_DIM
ATT_KV_W = ATT_KV_HEADS * ATT_HEAD_DIM
M_QK_W = MLSTM_HEADS * MLSTM_QK_DIM
M_V_W = MLSTM_HEADS * MLSTM_V_DIM
M_GATE_W = 4 * MLSTM_HEADS

OFF_AQ = 0
OFF_AK = OFF_AQ + ATT_Q_W
OFF_AV = OFF_AK + ATT_KV_W
OFF_MQ = OFF_AV + ATT_KV_W
OFF_MV = OFF_MQ + 2 * M_QK_W
OFF_MO = OFF_MV + M_V_W
OFF_MG = OFF_MO + M_V_W
OFF_BG = OFF_MG + M_GATE_W
IN_WIDTH = OFF_BG + 2 * D_MODEL

LANES = 128
SUBLANES = 8
BF16_ROWS = 16
VMEM_LIMIT = 56 * 1024 * 1024

SCAN_CHUNK = 256
N_STATE = 2 * MLSTM_HEADS
V_AUG = MLSTM_V_DIM + LANES
CONV_HALO = BF16_ROWS

LOG2E = 1.4426950408889634

BF16 = jnp.bfloat16
F32 = jnp.float32


def _rms(x, g):
    return x * lax.rsqrt(jnp.mean(x * x, axis=-1, keepdims=True) + RMS_EPS) * g


def _sigmoid(x):
    return 1.0 / (1.0 + jnp.exp(-x))


def _dot(a, b):
    return jnp.dot(a, b, preferred_element_type=F32)


def _dot_nt(a, b):
    return lax.dot_general(a, b, (((1,), (1,)), ((), ())), preferred_element_type=F32)


def _ones_column_block(rows, width, col=0):
    return jnp.where(lax.broadcasted_iota(jnp.int32, (rows, width), 1) == col, 1.0, 0.0).astype(BF16)


def _ffn_kernel(sub, hid_chunk, with_ple, *refs):
    if with_ple:
        (h_ref, gpre_ref, w1_ref, w2_ref, gpost_ref, p_ref, gple_pre_ref, wpg_ref, wpp_ref,
         gple_post_ref, o_ref) = refs
    else:
        h_ref, gpre_ref, w1_ref, w2_ref, gpost_ref, o_ref = refs
    for s in range(h_ref.shape[0] // sub):
        rows = slice(s * sub, (s + 1) * sub)
        x = h_ref[rows, :]
        xn = _rms(x, gpre_ref[...]).astype(BF16)
        acc = None
        for c in range(0, FFN_HIDDEN, hid_chunk):
            z1 = _dot(xn, w1_ref[:, c:c + hid_chunk])
            z2 = _dot(xn, w1_ref[:, FFN_HIDDEN + c:FFN_HIDDEN + c + hid_chunk])
            part = _dot((z1 * _sigmoid(z1) * z2).astype(BF16), w2_ref[c:c + hid_chunk, :])
            acc = part if acc is None else acc + part
        h = x + MACARON * _rms(acc, gpost_ref[...])
        if with_ple:
            gate = _sigmoid(_dot(_rms(h, gple_pre_ref[...]).astype(BF16), wpg_ref[...]))
            proj = _dot(p_ref[rows, :].astype(BF16), wpp_ref[...])
            h = h + _rms(proj * gate, gple_post_ref[...])
        o_ref[rows, :] = h


def _resident(shape):
    return pl.BlockSpec(shape, lambda *_: (0,) * len(shape), pipeline_mode=pl.Buffered(1))


def _ffn(h, g_pre, w1, w2, g_post, ple=None, *, tm=1024, sub=512, hid_chunk=1408, name="ffn"):
    t, d = h.shape
    row = lambda i: (i, 0)
    in_specs = [pl.BlockSpec((tm, d), row), _resident((1, d)), _resident(w1.shape), _resident(w2.shape),
                _resident((1, d))]
    args = [h, g_pre, w1, w2, g_post]
    if ple is not None:
        p, g_ple_pre, w_gate, w_proj, g_ple_post = ple
        in_specs += [pl.BlockSpec((tm, PLE_DIM), row), _resident((1, d)), _resident(w_gate.shape),
                     _resident(w_proj.shape), _resident((1, d))]
        args += [p, g_ple_pre, w_gate, w_proj, g_ple_post]
    return pl.pallas_call(
        functools.partial(_ffn_kernel, sub, hid_chunk, ple is not None),
        grid=(t // tm,),
        in_specs=in_specs,
        out_specs=pl.BlockSpec((tm, d), row),
        out_shape=jax.ShapeDtypeStruct((t, d), F32),
        compiler_params=pltpu.CompilerParams(dimension_semantics=("arbitrary",), vmem_limit_bytes=VMEM_LIMIT),
        name=name,
    )(*args)


WA_Q = 0
WA_KV = WA_Q + ATT_Q_W
WA_MQK = WA_KV + 2 * ATT_KV_W
WA_MV = WA_MQK + 2 * M_QK_W
WA_G = WA_MV + M_V_W
WA_END = WA_G + LANES


def _lane_scan(x, op, ident, reverse):
    width = x.shape[1]
    lane = lax.broadcasted_iota(jnp.int32, x.shape, 1)
    s = 1
    while s < width:
        if reverse:
            shifted = jnp.where(lane < width - s, pltpu.roll(x, width - s, 1), ident)
        else:
            shifted = jnp.where(lane >= s, pltpu.roll(x, s, 1), ident)
        x = op(x, shifted)
        s *= 2
    return x


def _inproj_kernel(n_tile, h_ref, hp_ref, hn_ref, g_ref, w_ref, bg_ref, rc_ref, rs1_ref, rs2_ref, cw_ref, cb_ref,
                   q_ref, k_ref, v_ref, mq_ref, mkt_ref, mv_ref, gcm_ref, gb_ref, grow_ref, ue_ref, ze_ref):
    i = pl.program_id(0)
    tm = h_ref.shape[0]
    g_pre = g_ref[...]
    halo = CONV_HALO
    ue_ref[0:halo, :] = jnp.where(i > 0, _rms(hp_ref[...], g_pre), 0.0).astype(BF16)
    ue_ref[halo:halo + tm, :] = _rms(h_ref[...], g_pre).astype(BF16)
    ue_ref[halo + tm:, :] = jnp.where(i < n_tile - 1, _rms(hn_ref[...], g_pre), 0.0).astype(BF16)
    u = ue_ref[halo:halo + tm, :]
    rc, rs1, rs2 = rc_ref[...], rs1_ref[...], rs2_ref[...]

    def rope(x):
        half = ROPE_DIM // 2
        return x * rc + pltpu.roll(x, LANES - half, 1) * rs1 + pltpu.roll(x, half, 1) * rs2

    zq = _dot(u, w_ref[:, WA_Q:WA_KV])
    for c in range(ATT_Q_W // LANES):
        q_ref[:, c * LANES:(c + 1) * LANES] = (
            rope(zq[:, c * LANES:(c + 1) * LANES]) * (ATT_HEAD_DIM ** -0.5 * LOG2E)).astype(BF16)
    zkv = _dot(u, w_ref[:, WA_KV:WA_MQK])
    k_ref[...] = rope(zkv[:, :ATT_KV_W]).astype(BF16)
    v_ref[...] = zkv[:, ATT_KV_W:].astype(BF16)
    mv_ref[...] = _dot(u, w_ref[:, WA_MV:WA_G]).astype(BF16)

    ze_ref[...] = _dot(ue_ref[...], w_ref[:, WA_MQK:WA_MV])
    cw = cw_ref[...]
    y = cb_ref[...]
    for j in range(MLSTM_CONV):
        off = halo - MLSTM_CONV // 2 + j
        y = y + ze_ref[off:off + tm, :] * cw[j:j + 1, :]
    qk = y * _sigmoid(y)
    mq_ref[...] = qk[:, :M_QK_W].astype(BF16)
    mkt_ref[...] = (qk[:, M_QK_W:] * (MLSTM_QK_DIM ** -0.5)).T.astype(BF16)

    gt = (_dot(u, w_ref[:, WA_G:WA_END]) + bg_ref[...]).T
    li = gt[0:N_STATE]
    f_raw = gt[N_STATE:2 * N_STATE]
    lf = jnp.minimum(f_raw, 0.0) - jnp.log1p(jnp.exp(-jnp.abs(f_raw)))
    is_fwd = lax.broadcasted_iota(jnp.int32, (N_STATE, SCAN_CHUNK), 0) < MLSTM_HEADS
    a_parts, cm_parts, b_parts = [], [], []
    for c in range(tm // SCAN_CHUNK):
        sl = slice(c * SCAN_CHUNK, (c + 1) * SCAN_CHUNK)
        lfc = lf[:, sl]
        b = jnp.where(is_fwd, _lane_scan(lfc, jnp.add, 0.0, False), _lane_scan(lfc, jnp.add, 0.0, True))
        a = li[:, sl] - b
        cm = jnp.where(is_fwd, _lane_scan(a, jnp.maximum, -jnp.inf, False),
                       _lane_scan(a, jnp.maximum, -jnp.inf, True))
        a_parts.append(a)
        cm_parts.append(cm)
        b_parts.append(b)
    cat = lambda parts: jnp.concatenate(parts, axis=1)
    grow_ref[...] = cat(a_parts)
    pad = jnp.zeros((LANES - N_STATE, tm), F32)
    gcm_ref[...] = jnp.concatenate([cat(cm_parts), pad], axis=0).T
    gb_ref[...] = jnp.concatenate([cat(b_parts), pad], axis=0).T


def _inproj(h3, g_pre, w_a, b_gate, rope_tabs, conv_w, conv_b, *, tm=512):
    b, s, d = h3.shape
    n_tile = s // tm
    hpt = tm // CONV_HALO
    n_halo = s // CONV_HALO
    tok = lambda i, bb: (bb, i, 0)
    tok_p = lambda i, bb: (bb, jnp.maximum(i * hpt - 1, 0), 0)
    tok_n = lambda i, bb: (bb, jnp.minimum((i + 1) * hpt, n_halo - 1), 0)
    const = lambda i, bb: (0, 0)
    tab = lambda i, bb: (i, 0)
    qk_w = 2 * M_QK_W
    tok_spec = lambda w: pl.BlockSpec((None, tm, w), tok)
    tr_spec = lambda r: pl.BlockSpec((None, r, tm), lambda i, bb: (bb, 0, i))
    return pl.pallas_call(
        functools.partial(_inproj_kernel, n_tile),
        grid=(n_tile, b),
        in_specs=[
            tok_spec(d),
            pl.BlockSpec((None, CONV_HALO, d), tok_p),
            pl.BlockSpec((None, CONV_HALO, d), tok_n),
            pl.BlockSpec((1, d), const),
            pl.BlockSpec((d, WA_END), const),
            pl.BlockSpec((1, LANES), const),
            pl.BlockSpec((tm, LANES), tab),
            pl.BlockSpec((tm, LANES), tab),
            pl.BlockSpec((tm, LANES), tab),
            pl.BlockSpec((MLSTM_CONV, qk_w), const),
            pl.BlockSpec((1, qk_w), const),
        ],
        out_specs=[tok_spec(ATT_Q_W), tok_spec(ATT_KV_W), tok_spec(ATT_KV_W), tok_spec(M_QK_W), tr_spec(M_QK_W),
                   tok_spec(M_V_W), tok_spec(LANES), tok_spec(LANES), tr_spec(N_STATE)],
        out_shape=[
            jax.ShapeDtypeStruct((b, s, ATT_Q_W), BF16),
            jax.ShapeDtypeStruct((b, s, ATT_KV_W), BF16),
            jax.ShapeDtypeStruct((b, s, ATT_KV_W), BF16),
            jax.ShapeDtypeStruct((b, s, M_QK_W), BF16),
            jax.ShapeDtypeStruct((b, M_QK_W, s), BF16),
            jax.ShapeDtypeStruct((b, s, M_V_W), BF16),
            jax.ShapeDtypeStruct((b, s, LANES), F32),
            jax.ShapeDtypeStruct((b, s, LANES), F32),
            jax.ShapeDtypeStruct((b, N_STATE, s), F32),
        ],
        scratch_shapes=[pltpu.VMEM((tm + 2 * CONV_HALO, d), BF16), pltpu.VMEM((tm + 2 * CONV_HALO, qk_w), F32)],
        compiler_params=pltpu.CompilerParams(
            dimension_semantics=("arbitrary", "arbitrary"), vmem_limit_bytes=VMEM_LIMIT),
        name="inproj",
    )(h3, h3, h3, g_pre, w_a, b_gate, *rope_tabs, conv_w, conv_b)


def _attn_kernel(sink_ref, q_ref, kp_ref, kc_ref, kn_ref, vp_ref, vc_ref, vn_ref, o_ref):
    i = pl.program_id(1)
    blk = ATT_BLOCK
    tq = q_ref.shape[0]
    dh = ATT_HEAD_DIM
    group = ATT_HEADS // ATT_KV_HEADS
    rows = group * blk
    n_sub = tq // blk
    qi = lax.broadcasted_iota(jnp.int32, (rows, blk), 0) % blk
    kc = lax.broadcasted_iota(jnp.int32, (rows, blk), 1)
    bias_prev = jnp.where(kc >= qi, 0.0, -jnp.inf)
    bias_next = jnp.where(kc <= qi, 0.0, -jnp.inf)
    first_prev = jnp.where(i == 0, -jnp.inf, bias_prev)
    last_next = jnp.where(i == pl.num_programs(1) - 1, -jnp.inf, bias_next)
    head_of_row = lax.broadcasted_iota(jnp.int32, (rows, 1), 0) // blk
    kcat = jnp.concatenate([kp_ref[...], kc_ref[...], kn_ref[...]], axis=0)
    vcat = jnp.concatenate([vp_ref[...], vc_ref[...], vn_ref[...]], axis=0)
    ones_blk = _ones_column_block(3 * blk, dh)
    for sb in range(n_sub):
        bp = first_prev if sb == 0 else bias_prev
        bn = last_next if sb == n_sub - 1 else bias_next
        outs = []
        for g in range(ATT_KV_HEADS):
            qs = jnp.concatenate(
                [q_ref[sb * blk:(sb + 1) * blk, (g * group + hh) * dh:(g * group + hh + 1) * dh]
                 for hh in range(group)], axis=0)
            kk = kcat[sb * blk:(sb + 3) * blk, g * dh:(g + 1) * dh]
            vv = jnp.concatenate([vcat[sb * blk:(sb + 3) * blk, g * dh:(g + 1) * dh], ones_blk], axis=1)
            sink = jnp.zeros((rows, 1), F32)
            for hh in range(group):
                sink = jnp.where(head_of_row == hh, sink_ref[g * group + hh] * LOG2E, sink)
            s = _dot_nt(qs, kk)
            s = jnp.concatenate([s[:, :blk] + bp, s[:, blk:2 * blk], s[:, 2 * blk:] + bn], axis=1)
            mx = jnp.maximum(jnp.max(s, axis=-1, keepdims=True), sink)
            p = jnp.exp2(s - mx)
            pv = _dot(p.astype(BF16), vv)
            o = pv[:, :dh] / (pv[:, dh:dh + 1] + jnp.exp2(sink - mx))
            outs += [o[hh * blk:(hh + 1) * blk] for hh in range(group)]
        o_ref[sb * blk:(sb + 1) * blk, :] = jnp.concatenate(outs, axis=-1).astype(BF16)


def _attention(q, k, v, sink, *, tq=512):
    b, s, _ = q.shape
    n_tile = s // tq
    bpt = tq // ATT_BLOCK
    n_blk = s // ATT_BLOCK
    cur = lambda bb, i: (bb, i, 0)
    prev = lambda bb, i: (bb, jnp.maximum(i * bpt - 1, 0), 0)
    nxt = lambda bb, i: (bb, jnp.minimum((i + 1) * bpt, n_blk - 1), 0)
    halo_spec = lambda im: pl.BlockSpec((None, ATT_BLOCK, ATT_KV_W), im)
    main_spec = pl.BlockSpec((None, tq, ATT_KV_W), cur)
    return pl.pallas_call(
        _attn_kernel,
        grid=(b, n_tile),
        in_specs=[
            pl.BlockSpec(memory_space=pltpu.SMEM),
            pl.BlockSpec((None, tq, ATT_Q_W), cur),
            halo_spec(prev), main_spec, halo_spec(nxt),
            halo_spec(prev), main_spec, halo_spec(nxt),
        ],
        out_specs=pl.BlockSpec((None, tq, ATT_Q_W), cur),
        out_shape=jax.ShapeDtypeStruct((b, s, ATT_Q_W), BF16),
        compiler_params=pltpu.CompilerParams(
            dimension_semantics=("arbitrary", "arbitrary"), vmem_limit_bytes=VMEM_LIMIT),
        name="attn",
    )(sink, q, k, k, k, v, v, v)


def _mlstm_kernel(qf_ref, ktf_ref, vf_ref, gmf_ref, gbf_ref, grf_ref,
                  qb_ref, ktb_ref, vb_ref, gmb_ref, gbb_ref, grb_ref,
                  of_ref, ob_ref, c_ref, m_ref):
    c = pl.program_id(1)
    L = SCAN_CHUNK
    dk = MLSTM_QK_DIM
    dv = MLSTM_V_DIM

    @pl.when(c == 0)
    def _():
        c_ref[...] = jnp.zeros_like(c_ref)
        m_ref[...] = jnp.zeros_like(m_ref)

    row = lax.broadcasted_iota(jnp.int32, (L, L), 0)
    col = lax.broadcasted_iota(jnp.int32, (L, L), 1)
    dirs = (
        (0, qf_ref, ktf_ref, vf_ref, gmf_ref, gbf_ref, grf_ref, of_ref),
        (1, qb_ref, ktb_ref, vb_ref, gmb_ref, gbb_ref, grb_ref, ob_ref),
    )
    for d, q_ref, kt_ref, v_ref, gm_ref, gb_ref, gr_ref, o_ref in dirs:
        causal = (col <= row) if d == 0 else (col >= row)
        last = L - 1 if d == 0 else 0
        m_prev_v = m_ref[d:d + 1, :]
        m_t = jnp.maximum(gm_ref[...], m_prev_v)
        w_inter_t = jnp.exp(m_prev_v - m_t)
        den_min_t = jnp.exp(-(gb_ref[...] + m_t))
        m_last_v = m_t[last:last + 1, :]
        decay_v = jnp.exp(m_prev_v - m_last_v)
        m_ref[d:d + 1, :] = gb_ref[last:last + 1, :] + m_last_v
        for h in range(MLSTM_HEADS):
            r = d * MLSTM_HEADS + h
            q = q_ref[:, h * dk:(h + 1) * dk]
            kt = kt_ref[h * dk:(h + 1) * dk, :]
            v_aug = jnp.concatenate(
                [v_ref[:, h * dv:(h + 1) * dv], _ones_column_block(L, LANES, r)], axis=1)
            a_r = gr_ref[r:r + 1, :]
            c_prev = c_ref[r]

            w_intra = jnp.exp(jnp.where(causal, a_r - m_t[:, r:r + 1], -jnp.inf))
            sc = (_dot(q, kt) * w_intra).astype(BF16)
            q_inter = (q.astype(F32) * w_inter_t[:, r:r + 1]).astype(BF16)
            tot = _dot(jnp.concatenate([sc, q_inter], axis=1),
                       jnp.concatenate([v_aug, c_prev.astype(BF16)], axis=0))
            den = jnp.maximum(jnp.abs(tot[:, dv:]), den_min_t)
            o_ref[:, h * dv:(h + 1) * dv] = (tot[:, :dv] * (1.0 / den)[:, r:r + 1]).astype(BF16)

            w_k = jnp.exp(a_r - m_last_v[:, r:r + 1])
            kw = (kt.astype(F32) * w_k).astype(BF16)
            c_ref[r] = decay_v[:, r:r + 1] * c_prev + _dot(kw, v_aug)


def _mlstm(mq, mkt, mv, gcm, gb, grow):
    b, s, _ = mq.shape
    L = SCAN_CHUNK
    n_chunk = s // L
    fwd = lambda bb, c: (bb, c, 0)
    bwd = lambda bb, c: (bb, n_chunk - 1 - c, 0)
    fwd_t = lambda bb, c: (bb, 0, c)
    bwd_t = lambda bb, c: (bb, 0, n_chunk - 1 - c)

    def dir_specs(tok, tr):
        return [
            pl.BlockSpec((None, L, M_QK_W), tok),
            pl.BlockSpec((None, M_QK_W, L), tr),
            pl.BlockSpec((None, L, M_V_W), tok),
            pl.BlockSpec((None, L, LANES), tok),
            pl.BlockSpec((None, L, LANES), tok),
            pl.BlockSpec((None, N_STATE, L), tr),
        ]

    return pl.pallas_call(
        _mlstm_kernel,
        grid=(b, n_chunk),
        in_specs=dir_specs(fwd, fwd_t) + dir_specs(bwd, bwd_t),
        out_specs=[pl.BlockSpec((None, L, M_V_W), fwd), pl.BlockSpec((None, L, M_V_W), bwd)],
        out_shape=[jax.ShapeDtypeStruct((b, s, M_V_W), BF16)] * 2,
        scratch_shapes=[
            pltpu.VMEM((N_STATE, MLSTM_QK_DIM, V_AUG), F32),
            pltpu.VMEM((SUBLANES, LANES), F32),
        ],
        compiler_params=pltpu.CompilerParams(
            dimension_semantics=("arbitrary", "arbitrary"), vmem_limit_bytes=VMEM_LIMIT),
        name="mlstm",
    )(mq, mkt, mv, gcm, gb, grow, mq, mkt, mv, gcm, gb, grow)


def _merge_kernel(sub, h_ref, oa_ref, hf_ref, hb_ref, gpre_ref, wmo_ref, wbg_ref, wba_ref, wbm_ref, wout_ref,
                  gm_ref, gpost_ref, o_ref):
    dv = MLSTM_V_DIM
    for s in range(h_ref.shape[0] // sub):
        rows = slice(s * sub, (s + 1) * sub)
        h = h_ref[rows, :]
        u = _rms(h, gpre_ref[...]).astype(BF16)
        hm = hf_ref[rows, :].astype(F32) + hb_ref[rows, :].astype(F32)
        hm = jnp.concatenate(
            [hm[:, i * dv:(i + 1) * dv]
             * lax.rsqrt(jnp.mean(hm[:, i * dv:(i + 1) * dv] ** 2, axis=-1, keepdims=True) + RMS_EPS)
             for i in range(MLSTM_HEADS)], axis=-1)
        hm = hm * gm_ref[...] * _sigmoid(_dot(u, wmo_ref[...]))
        ym = _dot(hm.astype(BF16), wbm_ref[...])
        ya = _dot(oa_ref[rows, :], wba_ref[...])
        merged = (_sigmoid(_dot(u, wbg_ref[:, :D_MODEL])) * ya
                  + _sigmoid(_dot(u, wbg_ref[:, D_MODEL:])) * ym)
        mix = _dot(merged.astype(BF16), wout_ref[...])
        o_ref[rows, :] = h + _rms(mix, gpost_ref[...])


def _merge(h, oa, hf, hb, g_pre, w_mo, w_bg, w_ba, w_bm, w_out, g_m, g_post, *, tm=1024, sub=512):
    t, d = h.shape
    row = lambda i: (i, 0)
    full = lambda a: _resident(a.shape)
    return pl.pallas_call(
        functools.partial(_merge_kernel, sub),
        grid=(t // tm,),
        in_specs=[
            pl.BlockSpec((tm, d), row), pl.BlockSpec((tm, ATT_Q_W), row),
            pl.BlockSpec((tm, M_V_W), row), pl.BlockSpec((tm, M_V_W), row),
            full(g_pre), full(w_mo), full(w_bg), full(w_ba), full(w_bm), full(w_out), full(g_m), full(g_post),
        ],
        out_specs=pl.BlockSpec((tm, d), row),
        out_shape=jax.ShapeDtypeStruct((t, d), F32),
        compiler_params=pltpu.CompilerParams(
            dimension_semantics=("arbitrary",), vmem_limit_bytes=VMEM_LIMIT),
        name="merge",
    )(h, oa, hf, hb, g_pre, w_mo, w_bg, w_ba, w_bm, w_out, g_m, g_post)


def _rope_tables(seq):
    pos = jnp.arange(seq, dtype=F32)
    inv = ROPE_THETA ** (-jnp.arange(0, ROPE_DIM, 2, dtype=F32) / ROPE_DIM)
    ang = pos[:, None] * inv[None, :]
    cos, sin = jnp.cos(ang), jnp.sin(ang)
    half = ROPE_DIM // 2
    dpos = jnp.arange(LANES) % ATT_HEAD_DIM
    sel = dpos % half
    rc = jnp.where(dpos < ROPE_DIM, cos[:, sel], 1.0)
    rs1 = jnp.where(dpos < half, -sin[:, sel], 0.0)
    rs2 = jnp.where((dpos >= half) & (dpos < ROPE_DIM), sin[:, sel], 0.0)
    return rc, rs1, rs2


def _gate_columns(a):
    g = a.reshape(a.shape[:-1] + (4, MLSTM_HEADS))
    g = jnp.stack([g[..., 0, :], g[..., 2, :], g[..., 1, :], g[..., 3, :]], axis=-2)
    g = g.reshape(a.shape[:-1] + (M_GATE_W,))
    return jnp.pad(g, [(0, 0)] * (a.ndim - 1) + [(0, LANES - M_GATE_W)])


def kernel(x, p, ffn1_norm_pre, ffn1_w1, ffn1_w2, ffn1_norm_post, mix_norm_pre, w_in, b_gates, conv_w, conv_b,
           attn_sink, mlstm_norm, w_branch_attn, w_branch_mlstm, w_out, mix_norm_post, ffn2_norm_pre, ffn2_w1,
           ffn2_w2, ffn2_norm_post, ple_norm_pre, w_ple_gate, w_ple_proj, ple_norm_post):
    bsz, seq, d = x.shape
    depth = p.shape[0]
    h = x.reshape(bsz * seq, d)
    rope_tabs = _rope_tables(seq)
    bf = lambda a: a.astype(BF16)
    for i in range(depth):
        h = _ffn(h, ffn1_norm_pre[i][None], bf(ffn1_w1[i]), bf(ffn1_w2[i]), ffn1_norm_post[i][None], name="ffn1")
        wi = w_in[i]
        w_a = bf(jnp.concatenate([wi[:, :OFF_MO], _gate_columns(wi[:, OFF_MG:OFF_BG])], axis=1))
        b_gate = _gate_columns(b_gates[i])[None]
        q, k, v, mq, mkt, mv, gcm, gb, grow = _inproj(
            h.reshape(bsz, seq, d), mix_norm_pre[i][None], w_a, b_gate, rope_tabs, conv_w[i], conv_b[i][None])
        oa = _attention(q, k, v, attn_sink[i])
        hf, hb = _mlstm(mq, mkt, mv, gcm, gb, grow)
        h = _merge(h, oa.reshape(bsz * seq, ATT_Q_W), hf.reshape(bsz * seq, M_V_W), hb.reshape(bsz * seq, M_V_W),
                   mix_norm_pre[i][None], bf(wi[:, OFF_MO:OFF_MG]), bf(wi[:, OFF_BG:]), bf(w_branch_attn[i]),
                   bf(w_branch_mlstm[i]), bf(w_out[i]), mlstm_norm[i][None], mix_norm_post[i][None])
        h = _ffn(h, ffn2_norm_pre[i][None], bf(ffn2_w1[i]), bf(ffn2_w2[i]), ffn2_norm_post[i][None],
                 ple=(p[i].reshape(bsz * seq, PLE_DIM), ple_norm_pre[i][None], bf(w_ple_gate[i]),
                      bf(w_ple_proj[i]), ple_norm_post[i][None]), name="ffn2_ple")
    return h.reshape(bsz, seq, d)
```

```python
import functools

import jax
import jax.numpy as jnp
from jax import lax
from jax.experimental import pallas as pl
from jax.experimental.pallas import tpu as pltpu

D_MODEL = 1024
PLE_DIM = 256
RMS_EPS = 1e-6
MACARON = 0.5
FFN_HIDDEN = 2816
ATT_HEADS = 8
ATT_KV_HEADS = 2
ATT_HEAD_DIM = 64
ATT_WINDOW = 128
ATT_BLOCK = 128
ROPE_THETA = 500000.0
ROPE_DIM = ATT_HEAD_DIM // 4
MLSTM_HEADS = 4
MLSTM_QK_DIM = 128
MLSTM_V_DIM = 256
MLSTM_CONV = 5
ATT_Q_W = ATT_HEADS * ATT_HEAD_DIM
ATT_KV_W = ATT_KV_HEADS * ATT_HEAD_DIM
M_QK_W = MLSTM_HEADS * MLSTM_QK_DIM
M_V_W = MLSTM_HEADS * MLSTM_V_DIM
M_GATE_W = 4 * MLSTM_HEADS

OFF_AQ = 0
OFF_AK = OFF_AQ + ATT_Q_W
OFF_AV = OFF_AK + ATT_KV_W
OFF_MQ = OFF_AV + ATT_KV_W
OFF_MV = OFF_MQ + 2 * M_QK_W
OFF_MO = OFF_MV + M_V_W
OFF_MG = OFF_MO + M_V_W
OFF_BG = OFF_MG + M_GATE_W
IN_WIDTH = OFF_BG + 2 * D_MODEL

LANES = 128
SUBLANES = 8
BF16_ROWS = 16
VMEM_LIMIT = 56 * 1024 * 1024

SCAN_CHUNK = 256
N_STATE = 2 * MLSTM_HEADS
V_AUG = MLSTM_V_DIM + LANES
CONV_HALO = BF16_ROWS

LOG2E = 1.4426950408889634

BF16 = jnp.bfloat16
F32 = jnp.float32


def _rms(x, g):
    return x * lax.rsqrt(jnp.mean(x * x, axis=-1, keepdims=True) + RMS_EPS) * g


def _sigmoid(x):
    return 1.0 / (1.0 + jnp.exp(-x))


def _dot(a, b):
    return jnp.dot(a, b, preferred_element_type=F32)


def _dot_nt(a, b):
    return lax.dot_general(a, b, (((1,), (1,)), ((), ())), preferred_element_type=F32)


def _ones_column_block(rows, width, col=0):
    return jnp.where(lax.broadcasted_iota(jnp.int32, (rows, width), 1) == col, 1.0, 0.0).astype(BF16)


def _ffn_kernel(sub, hid_chunk, with_ple, *refs):
    if with_ple:
        (h_ref, gpre_ref, w1_ref, w2_ref, gpost_ref, p_ref, gple_pre_ref, wpg_ref, wpp_ref,
         gple_post_ref, o_ref) = refs
    else:
        h_ref, gpre_ref, w1_ref, w2_ref, gpost_ref, o_ref = refs
    n_sub = h_ref.shape[0] // sub
    rows = [slice(s * sub, (s + 1) * sub) for s in range(n_sub)]
    xn = [_rms(h_ref[r, :], gpre_ref[...]).astype(BF16) for r in rows]

    def main(s):
        acc = None
        for c in range(0, FFN_HIDDEN, hid_chunk):
            z1 = _dot(xn[s], w1_ref[:, c:c + hid_chunk])
            z2 = _dot(xn[s], w1_ref[:, FFN_HIDDEN + c:FFN_HIDDEN + c + hid_chunk])
            part = _dot((z1 * _sigmoid(z1) * z2).astype(BF16), w2_ref[c:c + hid_chunk, :])
            acc = part if acc is None else acc + part
        return acc

    def epilogue(s, acc):
        h = h_ref[rows[s], :] + MACARON * _rms(acc, gpost_ref[...])
        if with_ple:
            gate = _sigmoid(_dot(_rms(h, gple_pre_ref[...]).astype(BF16), wpg_ref[...]))
            proj = _dot(p_ref[rows[s], :].astype(BF16), wpp_ref[...])
            h = h + _rms(proj * gate, gple_post_ref[...])
        o_ref[rows[s], :] = h

    prev = main(0)
    for s in range(1, n_sub):
        cur = main(s)
        epilogue(s - 1, prev)
        prev = cur
    epilogue(n_sub - 1, prev)


def _resident(shape):
    return pl.BlockSpec(shape, lambda *_: (0,) * len(shape), pipeline_mode=pl.Buffered(1))


def _ffn(h, g_pre, w1, w2, g_post, ple=None, *, tm=1024, sub=256, hid_chunk=2816, name="ffn"):
    t, d = h.shape
    row = lambda i: (i, 0)
    in_specs = [pl.BlockSpec((tm, d), row), _resident((1, d)), _resident(w1.shape), _resident(w2.shape),
                _resident((1, d))]
    args = [h, g_pre, w1, w2, g_post]
    if ple is not None:
        p, p_row0, g_ple_pre, w_gate, w_proj, g_ple_post = ple
        in_specs += [pl.BlockSpec((tm, PLE_DIM), lambda i: (i + p_row0 // tm, 0)), _resident((1, d)),
                     _resident(w_gate.shape),
                     _resident(w_proj.shape), _resident((1, d))]
        args += [p, g_ple_pre, w_gate, w_proj, g_ple_post]
    return pl.pallas_call(
        functools.partial(_ffn_kernel, sub, hid_chunk, ple is not None),
        grid=(t // tm,),
        in_specs=in_specs,
        out_specs=pl.BlockSpec((tm, d), row),
        out_shape=jax.ShapeDtypeStruct((t, d), F32),
        compiler_params=pltpu.CompilerParams(dimension_semantics=("arbitrary",), vmem_limit_bytes=VMEM_LIMIT),
        name=name,
    )(*args)


WA_Q = 0
WA_KV = WA_Q + ATT_Q_W
WA_MQK = WA_KV + 2 * ATT_KV_W
WA_MV = WA_MQK + 2 * M_QK_W
WA_G = WA_MV + M_V_W
WA_END = WA_G + LANES


def _lane_scan(x, op, ident, reverse):
    width = x.shape[1]
    lane = lax.broadcasted_iota(jnp.int32, x.shape, 1)
    s = 1
    while s < width:
        if reverse:
            shifted = jnp.where(lane < width - s, pltpu.roll(x, width - s, 1), ident)
        else:
            shifted = jnp.where(lane >= s, pltpu.roll(x, s, 1), ident)
        x = op(x, shifted)
        s *= 2
    return x


def _inproj_kernel(n_tile, sub, h_ref, hp_ref, hn_ref, g_ref, w_ref, bg_ref, rc_ref, rs1_ref, rs2_ref, cw_ref,
                   cb_ref, q_ref, k_ref, v_ref, mq_ref, mkt_ref, mv_ref, gcm_ref, gb_ref, grow_ref, ue_ref, ze_ref):
    i = pl.program_id(0)
    tm = h_ref.shape[0]
    g_pre = g_ref[...]
    halo = CONV_HALO
    ue_ref[0:halo, :] = jnp.where(i > 0, _rms(hp_ref[...], g_pre), 0.0).astype(BF16)
    ue_ref[halo:halo + tm, :] = _rms(h_ref[...], g_pre).astype(BF16)
    ue_ref[halo + tm:, :] = jnp.where(i < n_tile - 1, _rms(hn_ref[...], g_pre), 0.0).astype(BF16)
    is_fwd = lax.broadcasted_iota(jnp.int32, (N_STATE, SCAN_CHUNK), 0) < MLSTM_HEADS
    cat = lambda parts: jnp.concatenate(parts, axis=1)

    n_sub = tm // sub
    rows = [slice(s * sub, (s + 1) * sub) for s in range(n_sub)]
    u_of = lambda s: ue_ref[halo + s * sub:halo + (s + 1) * sub, :]

    def attn_proj(s):
        rc, rs1, rs2 = rc_ref[rows[s], :], rs1_ref[rows[s], :], rs2_ref[rows[s], :]

        def rope(x):
            half = ROPE_DIM // 2
            return x * rc + pltpu.roll(x, LANES - half, 1) * rs1 + pltpu.roll(x, half, 1) * rs2

        zq = _dot(u_of(s), w_ref[:, WA_Q:WA_KV])
        for c in range(ATT_Q_W // LANES):
            q_ref[rows[s], c * LANES:(c + 1) * LANES] = (
                rope(zq[:, c * LANES:(c + 1) * LANES]) * (ATT_HEAD_DIM ** -0.5 * LOG2E)).astype(BF16)
        zkv = _dot(u_of(s), w_ref[:, WA_KV:WA_MQK])
        k_ref[rows[s], :] = rope(zkv[:, :ATT_KV_W]).astype(BF16)
        v_ref[rows[s], :] = zkv[:, ATT_KV_W:].astype(BF16)

    def value_proj(s):
        mv_ref[rows[s], :] = _dot(u_of(s), w_ref[:, WA_MV:WA_G]).astype(BF16)

    def qk_proj(s):
        ze_ref[s] = _dot(ue_ref[s * sub:(s + 1) * sub + 2 * halo, :], w_ref[:, WA_MQK:WA_MV])

    def qk_conv(s):
        ze = ze_ref.at[s]
        cw = cw_ref[...]
        y = cb_ref[...]
        for j in range(MLSTM_CONV):
            off = halo - MLSTM_CONV // 2 + j
            y = y + ze[off:off + sub, :] * cw[j:j + 1, :]
        qk = y * _sigmoid(y)
        mq_ref[rows[s], :] = qk[:, :M_QK_W].astype(BF16)
        mkt_ref[:, rows[s]] = (qk[:, M_QK_W:] * (MLSTM_QK_DIM ** -0.5)).T.astype(BF16)

    for s in range(n_sub):
        attn_proj(s)
        value_proj(s)
        qk_proj(s)
        qk_conv(s)
        u = u_of(s)
        rows_s = rows[s]
        gt = (_dot(u, w_ref[:, WA_G:WA_END]) + bg_ref[...]).T
        li = gt[0:N_STATE]
        f_raw = gt[N_STATE:2 * N_STATE]
        lf = jnp.minimum(f_raw, 0.0) - jnp.log1p(jnp.exp(-jnp.abs(f_raw)))
        a_parts, cm_parts, b_parts = [], [], []
        for c in range(sub // SCAN_CHUNK):
            sl = slice(c * SCAN_CHUNK, (c + 1) * SCAN_CHUNK)
            lfc = lf[:, sl]
            b = jnp.where(is_fwd, _lane_scan(lfc, jnp.add, 0.0, False), _lane_scan(lfc, jnp.add, 0.0, True))
            a = li[:, sl] - b
            cm = jnp.where(is_fwd, _lane_scan(a, jnp.maximum, -jnp.inf, False),
                           _lane_scan(a, jnp.maximum, -jnp.inf, True))
            a_parts.append(a)
            cm_parts.append(cm)
            b_parts.append(b)
        grow_ref[:, rows_s] = cat(a_parts)
        pad = jnp.zeros((LANES - N_STATE, sub), F32)
        gcm_ref[rows_s, :] = jnp.concatenate([cat(cm_parts), pad], axis=0).T
        gb_ref[rows_s, :] = jnp.concatenate([cat(b_parts), pad], axis=0).T


def _inproj(h3, g_pre, w_a, b_gate, rope_tabs, conv_w, conv_b, *, tm=1024, sub=512):
    b, s, d = h3.shape
    n_tile = s // tm
    hpt = tm // CONV_HALO
    n_halo = s // CONV_HALO
    tok = lambda i, bb: (bb, i, 0)
    tok_p = lambda i, bb: (bb, jnp.maximum(i * hpt - 1, 0), 0)
    tok_n = lambda i, bb: (bb, jnp.minimum((i + 1) * hpt, n_halo - 1), 0)
    tab = lambda i, bb: (i, 0)
    qk_w = 2 * M_QK_W
    tok_spec = lambda w: pl.BlockSpec((None, tm, w), tok)
    tr_spec = lambda r: pl.BlockSpec((None, r, tm), lambda i, bb: (bb, 0, i))
    return pl.pallas_call(
        functools.partial(_inproj_kernel, n_tile, sub),
        grid=(n_tile, b),
        in_specs=[
            tok_spec(d),
            pl.BlockSpec((None, CONV_HALO, d), tok_p),
            pl.BlockSpec((None, CONV_HALO, d), tok_n),
            _resident((1, d)),
            _resident((d, WA_END)),
            _resident((1, LANES)),
            pl.BlockSpec((tm, LANES), tab),
            pl.BlockSpec((tm, LANES), tab),
            pl.BlockSpec((tm, LANES), tab),
            _resident((MLSTM_CONV, qk_w)),
            _resident((1, qk_w)),
        ],
        out_specs=[tok_spec(ATT_Q_W), tok_spec(ATT_KV_W), tok_spec(ATT_KV_W), tok_spec(M_QK_W), tr_spec(M_QK_W),
                   tok_spec(M_V_W), tok_spec(LANES), tok_spec(LANES), tr_spec(N_STATE)],
        out_shape=[
            jax.ShapeDtypeStruct((b, s, ATT_Q_W), BF16),
            jax.ShapeDtypeStruct((b, s, ATT_KV_W), BF16),
            jax.ShapeDtypeStruct((b, s, ATT_KV_W), BF16),
            jax.ShapeDtypeStruct((b, s, M_QK_W), BF16),
            jax.ShapeDtypeStruct((b, M_QK_W, s), BF16),
            jax.ShapeDtypeStruct((b, s, M_V_W), BF16),
            jax.ShapeDtypeStruct((b, s, LANES), F32),
            jax.ShapeDtypeStruct((b, s, LANES), F32),
            jax.ShapeDtypeStruct((b, N_STATE, s), F32),
        ],
        scratch_shapes=[pltpu.VMEM((tm + 2 * CONV_HALO, d), BF16),
                        pltpu.VMEM((tm // sub, sub + 2 * CONV_HALO, qk_w), F32)],
        compiler_params=pltpu.CompilerParams(
            dimension_semantics=("arbitrary", "arbitrary"), vmem_limit_bytes=VMEM_LIMIT),
        name="inproj",
    )(h3, h3, h3, g_pre, w_a, b_gate, *rope_tabs, conv_w, conv_b)


def _attn_kernel(sink_ref, q_ref, kp_ref, kc_ref, kn_ref, vp_ref, vc_ref, vn_ref, o_ref):
    i = pl.program_id(1)
    blk = ATT_BLOCK
    tq = q_ref.shape[0]
    dh = ATT_HEAD_DIM
    group = ATT_HEADS // ATT_KV_HEADS
    rows = group * blk
    n_sub = tq // blk
    qi = lax.broadcasted_iota(jnp.int32, (rows, blk), 0) % blk
    kc = lax.broadcasted_iota(jnp.int32, (rows, blk), 1)
    bias_prev = jnp.where(kc >= qi, 0.0, -jnp.inf)
    bias_next = jnp.where(kc <= qi, 0.0, -jnp.inf)
    first_prev = jnp.where(i == 0, -jnp.inf, bias_prev)
    last_next = jnp.where(i == pl.num_programs(1) - 1, -jnp.inf, bias_next)
    head_of_row = lax.broadcasted_iota(jnp.int32, (rows, 1), 0) // blk
    kcat = jnp.concatenate([kp_ref[...], kc_ref[...], kn_ref[...]], axis=0)
    vcat = jnp.concatenate([vp_ref[...], vc_ref[...], vn_ref[...]], axis=0)
    ones_blk = _ones_column_block(3 * blk, dh)
    for sb in range(n_sub):
        bp = first_prev if sb == 0 else bias_prev
        bn = last_next if sb == n_sub - 1 else bias_next
        outs = []
        for g in range(ATT_KV_HEADS):
            qs = jnp.concatenate(
                [q_ref[sb * blk:(sb + 1) * blk, (g * group + hh) * dh:(g * group + hh + 1) * dh]
                 for hh in range(group)], axis=0)
            kk = kcat[sb * blk:(sb + 3) * blk, g * dh:(g + 1) * dh]
            vv = jnp.concatenate([vcat[sb * blk:(sb + 3) * blk, g * dh:(g + 1) * dh], ones_blk], axis=1)
            sink = jnp.zeros((rows, 1), F32)
            for hh in range(group):
                sink = jnp.where(head_of_row == hh, sink_ref[g * group + hh] * LOG2E, sink)
            s = _dot_nt(qs, kk)
            s = jnp.concatenate([s[:, :blk] + bp, s[:, blk:2 * blk], s[:, 2 * blk:] + bn], axis=1)
            mx = jnp.maximum(jnp.max(s, axis=-1, keepdims=True), sink)
            p = jnp.exp2(s - mx)
            pv = _dot(p.astype(BF16), vv)
            o = pv[:, :dh] / (pv[:, dh:dh + 1] + jnp.exp2(sink - mx))
            outs += [o[hh * blk:(hh + 1) * blk] for hh in range(group)]
        o_ref[sb * blk:(sb + 1) * blk, :] = jnp.concatenate(outs, axis=-1).astype(BF16)


def _attention(q, k, v, sink, *, tq=512):
    b, s, _ = q.shape
    n_tile = s // tq
    bpt = tq // ATT_BLOCK
    n_blk = s // ATT_BLOCK
    cur = lambda bb, i: (bb, i, 0)
    prev = lambda bb, i: (bb, jnp.maximum(i * bpt - 1, 0), 0)
    nxt = lambda bb, i: (bb, jnp.minimum((i + 1) * bpt, n_blk - 1), 0)
    halo_spec = lambda im: pl.BlockSpec((None, ATT_BLOCK, ATT_KV_W), im)
    main_spec = pl.BlockSpec((None, tq, ATT_KV_W), cur)
    return pl.pallas_call(
        _attn_kernel,
        grid=(b, n_tile),
        in_specs=[
            pl.BlockSpec(memory_space=pltpu.SMEM),
            pl.BlockSpec((None, tq, ATT_Q_W), cur),
            halo_spec(prev), main_spec, halo_spec(nxt),
            halo_spec(prev), main_spec, halo_spec(nxt),
        ],
        out_specs=pl.BlockSpec((None, tq, ATT_Q_W), cur),
        out_shape=jax.ShapeDtypeStruct((b, s, ATT_Q_W), BF16),
        compiler_params=pltpu.CompilerParams(
            dimension_semantics=("arbitrary", "arbitrary"), vmem_limit_bytes=VMEM_LIMIT),
        name="attn",
    )(sink, q, k, k, k, v, v, v)


def _mlstm_kernel(qf_ref, ktf_ref, vf_ref, gmf_ref, gbf_ref, grf_ref,
                  qb_ref, ktb_ref, vb_ref, gmb_ref, gbb_ref, grb_ref,
                  of_ref, ob_ref, c_ref, m_ref):
    c = pl.program_id(1)
    L = SCAN_CHUNK
    dk = MLSTM_QK_DIM
    dv = MLSTM_V_DIM

    @pl.when(c == 0)
    def _():
        c_ref[...] = jnp.zeros_like(c_ref)
        m_ref[...] = jnp.zeros_like(m_ref)

    row = lax.broadcasted_iota(jnp.int32, (L, L), 0)
    col = lax.broadcasted_iota(jnp.int32, (L, L), 1)
    dirs = (
        (0, qf_ref, ktf_ref, vf_ref, gmf_ref, gbf_ref, grf_ref, of_ref),
        (1, qb_ref, ktb_ref, vb_ref, gmb_ref, gbb_ref, grb_ref, ob_ref),
    )
    for d, q_ref, kt_ref, v_ref, gm_ref, gb_ref, gr_ref, o_ref in dirs:
        causal = (col <= row) if d == 0 else (col >= row)
        last = L - 1 if d == 0 else 0
        m_prev_v = m_ref[d:d + 1, :]
        m_t = jnp.maximum(gm_ref[...], m_prev_v)
        w_inter_t = jnp.exp(m_prev_v - m_t)
        den_min_t = jnp.exp(-(gb_ref[...] + m_t))
        m_last_v = m_t[last:last + 1, :]
        decay_v = jnp.exp(m_prev_v - m_last_v)
        m_ref[d:d + 1, :] = gb_ref[last:last + 1, :] + m_last_v
        for h in range(MLSTM_HEADS):
            r = d * MLSTM_HEADS + h
            q = q_ref[:, h * dk:(h + 1) * dk]
            kt = kt_ref[h * dk:(h + 1) * dk, :]
            v_aug = jnp.concatenate(
                [v_ref[:, h * dv:(h + 1) * dv], _ones_column_block(L, LANES, r)], axis=1)
            a_r = gr_ref[r:r + 1, :]
            c_prev = c_ref[r]

            w_intra = jnp.exp(jnp.where(causal, a_r - m_t[:, r:r + 1], -jnp.inf))
            sc = (_dot(q, kt) * w_intra).astype(BF16)
            q_inter = (q.astype(F32) * w_inter_t[:, r:r + 1]).astype(BF16)
            tot = _dot(jnp.concatenate([sc, q_inter], axis=1),
                       jnp.concatenate([v_aug, c_prev.astype(BF16)], axis=0))
            den = jnp.maximum(jnp.abs(tot[:, dv:]), den_min_t)
            o_ref[:, h * dv:(h + 1) * dv] = (tot[:, :dv] * (1.0 / den)[:, r:r + 1]).astype(BF16)

            w_k = jnp.exp(a_r - m_last_v[:, r:r + 1])
            kw = (kt.astype(F32) * w_k).astype(BF16)
            c_ref[r] = decay_v[:, r:r + 1] * c_prev + _dot(kw, v_aug)


def _mlstm(mq, mkt, mv, gcm, gb, grow):
    b, s, _ = mq.shape
    L = SCAN_CHUNK
    n_chunk = s // L
    fwd = lambda bb, c: (bb, c, 0)
    bwd = lambda bb, c: (bb, n_chunk - 1 - c, 0)
    fwd_t = lambda bb, c: (bb, 0, c)
    bwd_t = lambda bb, c: (bb, 0, n_chunk - 1 - c)

    def dir_specs(tok, tr):
        return [
            pl.BlockSpec((None, L, M_QK_W), tok),
            pl.BlockSpec((None, M_QK_W, L), tr),
            pl.BlockSpec((None, L, M_V_W), tok),
            pl.BlockSpec((None, L, LANES), tok),
            pl.BlockSpec((None, L, LANES), tok),
            pl.BlockSpec((None, N_STATE, L), tr),
        ]

    return pl.pallas_call(
        _mlstm_kernel,
        grid=(b, n_chunk),
        in_specs=dir_specs(fwd, fwd_t) + dir_specs(bwd, bwd_t),
        out_specs=[pl.BlockSpec((None, L, M_V_W), fwd), pl.BlockSpec((None, L, M_V_W), bwd)],
        out_shape=[jax.ShapeDtypeStruct((b, s, M_V_W), BF16)] * 2,
        scratch_shapes=[
            pltpu.VMEM((N_STATE, MLSTM_QK_DIM, V_AUG), F32),
            pltpu.VMEM((SUBLANES, LANES), F32),
        ],
        compiler_params=pltpu.CompilerParams(
            dimension_semantics=("arbitrary", "arbitrary"), vmem_limit_bytes=VMEM_LIMIT),
        name="mlstm",
    )(mq, mkt, mv, gcm, gb, grow, mq, mkt, mv, gcm, gb, grow)


def _merge_kernel(sub, h_ref, oa_ref, hf_ref, hb_ref, gpre_ref, wmo_ref, wbg_ref, wba_ref, wbm_ref, wout_ref,
                  gm_ref, gpost_ref, o_ref):
    dv = MLSTM_V_DIM
    n_sub = h_ref.shape[0] // sub
    rows = [slice(s * sub, (s + 1) * sub) for s in range(n_sub)]

    def prologue(s):
        u = _rms(h_ref[rows[s], :], gpre_ref[...]).astype(BF16)
        hm = hf_ref[rows[s], :].astype(F32) + hb_ref[rows[s], :].astype(F32)
        hm = jnp.concatenate(
            [hm[:, i * dv:(i + 1) * dv]
             * lax.rsqrt(jnp.mean(hm[:, i * dv:(i + 1) * dv] ** 2, axis=-1, keepdims=True) + RMS_EPS)
             for i in range(MLSTM_HEADS)], axis=-1)
        return u, hm * gm_ref[...]

    def main(s, u, hm):
        hm = hm * _sigmoid(_dot(u, wmo_ref[...]))
        ym = _dot(hm.astype(BF16), wbm_ref[...])
        ya = _dot(oa_ref[rows[s], :], wba_ref[...])
        merged = (_sigmoid(_dot(u, wbg_ref[:, :D_MODEL])) * ya
                  + _sigmoid(_dot(u, wbg_ref[:, D_MODEL:])) * ym)
        return _dot(merged.astype(BF16), wout_ref[...])

    def epilogue(s, mix):
        o_ref[rows[s], :] = h_ref[rows[s], :] + _rms(mix, gpost_ref[...])

    pre = [prologue(s) for s in range(n_sub)]
    prev = main(0, *pre[0])
    for s in range(1, n_sub):
        cur = main(s, *pre[s])
        epilogue(s - 1, prev)
        prev = cur
    epilogue(n_sub - 1, prev)


def _merge(h, oa, hf, hb, g_pre, w_mo, w_bg, w_ba, w_bm, w_out, g_m, g_post, *, tm=1024, sub=256):
    t, d = h.shape
    row = lambda i: (i, 0)
    full = lambda a: _resident(a.shape)
    return pl.pallas_call(
        functools.partial(_merge_kernel, sub),
        grid=(t // tm,),
        in_specs=[
            pl.BlockSpec((tm, d), row), pl.BlockSpec((tm, ATT_Q_W), row),
            pl.BlockSpec((tm, M_V_W), row), pl.BlockSpec((tm, M_V_W), row),
            full(g_pre), full(w_mo), full(w_bg), full(w_ba), full(w_bm), full(w_out), full(g_m), full(g_post),
        ],
        out_specs=pl.BlockSpec((tm, d), row),
        out_shape=jax.ShapeDtypeStruct((t, d), F32),
        compiler_params=pltpu.CompilerParams(
            dimension_semantics=("arbitrary",), vmem_limit_bytes=VMEM_LIMIT),
        name="merge",
    )(h, oa, hf, hb, g_pre, w_mo, w_bg, w_ba, w_bm, w_out, g_m, g_post)


def _rope_tables(seq):
    pos = jnp.arange(seq, dtype=F32)
    inv = ROPE_THETA ** (-jnp.arange(0, ROPE_DIM, 2, dtype=F32) / ROPE_DIM)
    ang = pos[:, None] * inv[None, :]
    cos, sin = jnp.cos(ang), jnp.sin(ang)
    half = ROPE_DIM // 2
    dpos = jnp.arange(LANES) % ATT_HEAD_DIM
    sel = dpos % half
    rc = jnp.where(dpos < ROPE_DIM, cos[:, sel], 1.0)
    rs1 = jnp.where(dpos < half, -sin[:, sel], 0.0)
    rs2 = jnp.where((dpos >= half) & (dpos < ROPE_DIM), sin[:, sel], 0.0)
    return rc, rs1, rs2


def _gate_columns(a):
    g = a.reshape(a.shape[:-1] + (4, MLSTM_HEADS))
    g = jnp.stack([g[..., 0, :], g[..., 2, :], g[..., 1, :], g[..., 3, :]], axis=-2)
    g = g.reshape(a.shape[:-1] + (M_GATE_W,))
    return jnp.pad(g, [(0, 0)] * (a.ndim - 1) + [(0, LANES - M_GATE_W)])


def kernel(x, p, ffn1_norm_pre, ffn1_w1, ffn1_w2, ffn1_norm_post, mix_norm_pre, w_in, b_gates, conv_w, conv_b,
           attn_sink, mlstm_norm, w_branch_attn, w_branch_mlstm, w_out, mix_norm_post, ffn2_norm_pre, ffn2_w1,
           ffn2_w2, ffn2_norm_post, ple_norm_pre, w_ple_gate, w_ple_proj, ple_norm_post):
    bsz, seq, d = x.shape
    depth = p.shape[0]
    h = x.reshape(bsz * seq, d)
    rope_tabs = _rope_tables(seq)
    bf = lambda a: a.astype(BF16)
    for i in range(depth):
        h = _ffn(h, ffn1_norm_pre[i][None], bf(ffn1_w1[i]), bf(ffn1_w2[i]), ffn1_norm_post[i][None], name="ffn1")
        wi = w_in[i]
        w_a = bf(jnp.concatenate([wi[:, :OFF_MO], _gate_columns(wi[:, OFF_MG:OFF_BG])], axis=1))
        b_gate = _gate_columns(b_gates[i])[None]
        q, k, v, mq, mkt, mv, gcm, gb, grow = _inproj(
            h.reshape(bsz, seq, d), mix_norm_pre[i][None], w_a, b_gate, rope_tabs, conv_w[i], conv_b[i][None])
        oa = _attention(q, k, v, attn_sink[i])
        hf, hb = _mlstm(mq, mkt, mv, gcm, gb, grow)
        h = _merge(h, oa.reshape(bsz * seq, ATT_Q_W), hf.reshape(bsz * seq, M_V_W), hb.reshape(bsz * seq, M_V_W),
                   mix_norm_pre[i][None], bf(wi[:, OFF_MO:OFF_MG]), bf(wi[:, OFF_BG:]), bf(w_branch_attn[i]),
                   bf(w_branch_mlstm[i]), bf(w_out[i]), mlstm_norm[i][None], mix_norm_post[i][None])
        h = _ffn(h, ffn2_norm_pre[i][None], bf(ffn2_w1[i]), bf(ffn2_w2[i]), ffn2_norm_post[i][None],
                 ple=(p.reshape(depth * bsz * seq, PLE_DIM), i * bsz * seq, ple_norm_pre[i][None], bf(w_ple_gate[i]),
                      bf(w_ple_proj[i]), ple_norm_post[i][None]), name="ffn2_ple")
    return h.reshape(bsz, seq, d)
```

```python
import functools

import jax
import jax.numpy as jnp
from jax import lax
from jax.experimental import pallas as pl
from jax.experimental.pallas import tpu as pltpu

D_MODEL = 1024
PLE_DIM = 256
RMS_EPS = 1e-6
MACARON = 0.5
FFN_HIDDEN = 2816
ATT_HEADS = 8
ATT_KV_HEADS = 2
ATT_HEAD_DIM = 64
ATT_WINDOW = 128
ATT_BLOCK = 128
ROPE_THETA = 500000.0
ROPE_DIM = ATT_HEAD_DIM // 4
MLSTM_HEADS = 4
MLSTM_QK_DIM = 128
MLSTM_V_DIM = 256
MLSTM_CONV = 5
ATT_Q_W = ATT_HEADS * ATT_HEAD_DIM
ATT_KV_W = ATT_KV_HEADS * ATT_HEAD_DIM
M_QK_W = MLSTM_HEADS * MLSTM_QK_DIM
M_V_W = MLSTM_HEADS * MLSTM_V_DIM
M_GATE_W = 4 * MLSTM_HEADS

OFF_AQ = 0
OFF_AK = OFF_AQ + ATT_Q_W
OFF_AV = OFF_AK + ATT_KV_W
OFF_MQ = OFF_AV + ATT_KV_W
OFF_MV = OFF_MQ + 2 * M_QK_W
OFF_MO = OFF_MV + M_V_W
OFF_MG = OFF_MO + M_V_W
OFF_BG = OFF_MG + M_GATE_W
IN_WIDTH = OFF_BG + 2 * D_MODEL

LANES = 128
SUBLANES = 8
BF16_ROWS = 16
VMEM_LIMIT = 56 * 1024 * 1024

SCAN_CHUNK = 256
N_STATE = 2 * MLSTM_HEADS
V_AUG = MLSTM_V_DIM + LANES
CONV_HALO = BF16_ROWS

LOG2E = 1.4426950408889634
ATT_UNIT_SUBS = 8

BF16 = jnp.bfloat16
F32 = jnp.float32


def _rms(x, g):
    return x * lax.rsqrt(jnp.mean(x * x, axis=-1, keepdims=True) + RMS_EPS) * g


def _sigmoid(x):
    return 1.0 / (1.0 + jnp.exp(-x))


def _dot(a, b):
    return jnp.dot(a, b, preferred_element_type=F32)


def _dot_nt(a, b):
    return lax.dot_general(a, b, (((1,), (1,)), ((), ())), preferred_element_type=F32)


def _ones_column_block(rows, width, col=0):
    return jnp.where(lax.broadcasted_iota(jnp.int32, (rows, width), 1) == col, 1.0, 0.0).astype(BF16)


def _ffn_kernel(sub, hid_chunk, with_ple, *refs):
    if with_ple:
        (h_ref, gpre_ref, w1_ref, w2_ref, gpost_ref, p_ref, gple_pre_ref, wpg_ref, wpp_ref,
         gple_post_ref, o_ref) = refs
    else:
        h_ref, gpre_ref, w1_ref, w2_ref, gpost_ref, o_ref = refs
    n_sub = h_ref.shape[0] // sub
    rows = [slice(s * sub, (s + 1) * sub) for s in range(n_sub)]
    xn = [_rms(h_ref[r, :], gpre_ref[...]).astype(BF16) for r in rows]

    def main(s):
        acc = None
        for c in range(0, FFN_HIDDEN, hid_chunk):
            z1 = _dot(xn[s], w1_ref[:, c:c + hid_chunk])
            z2 = _dot(xn[s], w1_ref[:, FFN_HIDDEN + c:FFN_HIDDEN + c + hid_chunk])
            part = _dot((z1 * _sigmoid(z1) * z2).astype(BF16), w2_ref[c:c + hid_chunk, :])
            acc = part if acc is None else acc + part
        return acc

    def epilogue(s, acc):
        h = h_ref[rows[s], :] + MACARON * _rms(acc, gpost_ref[...])
        if with_ple:
            gate = _sigmoid(_dot(_rms(h, gple_pre_ref[...]).astype(BF16), wpg_ref[...]))
            proj = _dot(p_ref[rows[s], :].astype(BF16), wpp_ref[...])
            h = h + _rms(proj * gate, gple_post_ref[...])
        o_ref[rows[s], :] = h

    prev = main(0)
    for s in range(1, n_sub):
        cur = main(s)
        epilogue(s - 1, prev)
        prev = cur
    epilogue(n_sub - 1, prev)


def _resident(shape):
    return pl.BlockSpec(shape, lambda *_: (0,) * len(shape), pipeline_mode=pl.Buffered(1))


def _ffn(h, g_pre, w1, w2, g_post, ple=None, *, tm=1024, sub=256, hid_chunk=2816, name="ffn"):
    t, d = h.shape
    row = lambda i: (i, 0)
    in_specs = [pl.BlockSpec((tm, d), row), _resident((1, d)), _resident(w1.shape), _resident(w2.shape),
                _resident((1, d))]
    args = [h, g_pre, w1, w2, g_post]
    if ple is not None:
        p, p_row0, g_ple_pre, w_gate, w_proj, g_ple_post = ple
        in_specs += [pl.BlockSpec((tm, PLE_DIM), lambda i: (i + p_row0 // tm, 0)), _resident((1, d)),
                     _resident(w_gate.shape),
                     _resident(w_proj.shape), _resident((1, d))]
        args += [p, g_ple_pre, w_gate, w_proj, g_ple_post]
    return pl.pallas_call(
        functools.partial(_ffn_kernel, sub, hid_chunk, ple is not None),
        grid=(t // tm,),
        in_specs=in_specs,
        out_specs=pl.BlockSpec((tm, d), row),
        out_shape=jax.ShapeDtypeStruct((t, d), F32),
        compiler_params=pltpu.CompilerParams(dimension_semantics=("arbitrary",), vmem_limit_bytes=VMEM_LIMIT),
        name=name,
    )(*args)


WA_Q = 0
WA_KV = WA_Q + ATT_Q_W
WA_MQK = WA_KV + 2 * ATT_KV_W
WA_MV = WA_MQK + 2 * M_QK_W
WA_G = WA_MV + M_V_W
WA_END = WA_G + LANES


def _lane_scan(x, op, ident, reverse):
    width = x.shape[1]
    lane = lax.broadcasted_iota(jnp.int32, x.shape, 1)
    s = 1
    while s < width:
        if reverse:
            shifted = jnp.where(lane < width - s, pltpu.roll(x, width - s, 1), ident)
        else:
            shifted = jnp.where(lane >= s, pltpu.roll(x, s, 1), ident)
        x = op(x, shifted)
        s *= 2
    return x


def _inproj_kernel(n_tile, sub, h_ref, hp_ref, hn_ref, g_ref, w_ref, bg_ref, rc_ref, rs1_ref, rs2_ref, cw_ref,
                   cb_ref, q_ref, k_ref, v_ref, mq_ref, mkt_ref, mv_ref, gcm_ref, gb_ref, grow_ref, ue_ref, ze_ref):
    i = pl.program_id(0)
    tm = h_ref.shape[0]
    g_pre = g_ref[...]
    halo = CONV_HALO
    ue_ref[0:halo, :] = jnp.where(i > 0, _rms(hp_ref[...], g_pre), 0.0).astype(BF16)
    ue_ref[halo:halo + tm, :] = _rms(h_ref[...], g_pre).astype(BF16)
    ue_ref[halo + tm:, :] = jnp.where(i < n_tile - 1, _rms(hn_ref[...], g_pre), 0.0).astype(BF16)
    is_fwd = lax.broadcasted_iota(jnp.int32, (N_STATE, SCAN_CHUNK), 0) < MLSTM_HEADS
    cat = lambda parts: jnp.concatenate(parts, axis=1)

    n_sub = tm // sub
    rows = [slice(s * sub, (s + 1) * sub) for s in range(n_sub)]
    u_of = lambda s: ue_ref[halo + s * sub:halo + (s + 1) * sub, :]

    def attn_proj(s):
        rc, rs1, rs2 = rc_ref[rows[s], :], rs1_ref[rows[s], :], rs2_ref[rows[s], :]

        def rope(x):
            half = ROPE_DIM // 2
            return x * rc + pltpu.roll(x, LANES - half, 1) * rs1 + pltpu.roll(x, half, 1) * rs2

        zq = _dot(u_of(s), w_ref[:, WA_Q:WA_KV])
        for c in range(ATT_Q_W // LANES):
            q_ref[rows[s], c * LANES:(c + 1) * LANES] = (
                rope(zq[:, c * LANES:(c + 1) * LANES]) * (ATT_HEAD_DIM ** -0.5 * LOG2E)).astype(BF16)
        zkv = _dot(u_of(s), w_ref[:, WA_KV:WA_MQK])
        k_ref[rows[s], :] = rope(zkv[:, :ATT_KV_W]).astype(BF16)
        v_ref[rows[s], :] = zkv[:, ATT_KV_W:].astype(BF16)

    def value_proj(s):
        mv_ref[rows[s], :] = _dot(u_of(s), w_ref[:, WA_MV:WA_G]).astype(BF16)

    def qk_proj(s):
        ze_ref[s] = _dot(ue_ref[s * sub:(s + 1) * sub + 2 * halo, :], w_ref[:, WA_MQK:WA_MV])

    def qk_conv(s):
        ze = ze_ref.at[s]
        cw = cw_ref[...]
        y = cb_ref[...]
        for j in range(MLSTM_CONV):
            off = halo - MLSTM_CONV // 2 + j
            y = y + ze[off:off + sub, :] * cw[j:j + 1, :]
        qk = y * _sigmoid(y)
        mq_ref[rows[s], :] = qk[:, :M_QK_W].astype(BF16)
        mkt_ref[:, rows[s]] = (qk[:, M_QK_W:] * (MLSTM_QK_DIM ** -0.5)).T.astype(BF16)

    for s in range(n_sub):
        attn_proj(s)
        value_proj(s)
        qk_proj(s)
        qk_conv(s)
        u = u_of(s)
        rows_s = rows[s]
        gt = (_dot(u, w_ref[:, WA_G:WA_END]) + bg_ref[...]).T
        li = gt[0:N_STATE]
        f_raw = gt[N_STATE:2 * N_STATE]
        lf = jnp.minimum(f_raw, 0.0) - jnp.log1p(jnp.exp(-jnp.abs(f_raw)))
        a_parts, cm_parts, b_parts = [], [], []
        for c in range(sub // SCAN_CHUNK):
            sl = slice(c * SCAN_CHUNK, (c + 1) * SCAN_CHUNK)
            lfc = lf[:, sl]
            b = jnp.where(is_fwd, _lane_scan(lfc, jnp.add, 0.0, False), _lane_scan(lfc, jnp.add, 0.0, True))
            a = li[:, sl] - b
            cm = jnp.where(is_fwd, _lane_scan(a, jnp.maximum, -jnp.inf, False),
                           _lane_scan(a, jnp.maximum, -jnp.inf, True))
            a_parts.append(a)
            cm_parts.append(cm)
            b_parts.append(b)
        grow_ref[:, rows_s] = cat(a_parts)
        pad = jnp.zeros((LANES - N_STATE, sub), F32)
        gcm_ref[rows_s, :] = jnp.concatenate([cat(cm_parts), pad], axis=0).T
        gb_ref[rows_s, :] = jnp.concatenate([cat(b_parts), pad], axis=0).T


def _inproj(h3, g_pre, w_a, b_gate, rope_tabs, conv_w, conv_b, *, tm=1024, sub=512):
    b, s, d = h3.shape
    n_tile = s // tm
    hpt = tm // CONV_HALO
    n_halo = s // CONV_HALO
    tok = lambda i, bb: (bb, i, 0)
    tok_p = lambda i, bb: (bb, jnp.maximum(i * hpt - 1, 0), 0)
    tok_n = lambda i, bb: (bb, jnp.minimum((i + 1) * hpt, n_halo - 1), 0)
    tab = lambda i, bb: (i, 0)
    qk_w = 2 * M_QK_W
    tok_spec = lambda w: pl.BlockSpec((None, tm, w), tok)
    tr_spec = lambda r: pl.BlockSpec((None, r, tm), lambda i, bb: (bb, 0, i))
    return pl.pallas_call(
        functools.partial(_inproj_kernel, n_tile, sub),
        grid=(n_tile, b),
        in_specs=[
            tok_spec(d),
            pl.BlockSpec((None, CONV_HALO, d), tok_p),
            pl.BlockSpec((None, CONV_HALO, d), tok_n),
            _resident((1, d)),
            _resident((d, WA_END)),
            _resident((1, LANES)),
            pl.BlockSpec((tm, LANES), tab),
            pl.BlockSpec((tm, LANES), tab),
            pl.BlockSpec((tm, LANES), tab),
            _resident((MLSTM_CONV, qk_w)),
            _resident((1, qk_w)),
        ],
        out_specs=[tok_spec(ATT_Q_W), tok_spec(ATT_KV_W), tok_spec(ATT_KV_W), tok_spec(M_QK_W), tr_spec(M_QK_W),
                   tok_spec(M_V_W), tok_spec(LANES), tok_spec(LANES), tr_spec(N_STATE)],
        out_shape=[
            jax.ShapeDtypeStruct((b, s, ATT_Q_W), BF16),
            jax.ShapeDtypeStruct((b, s, ATT_KV_W), BF16),
            jax.ShapeDtypeStruct((b, s, ATT_KV_W), BF16),
            jax.ShapeDtypeStruct((b, s, M_QK_W), BF16),
            jax.ShapeDtypeStruct((b, M_QK_W, s), BF16),
            jax.ShapeDtypeStruct((b, s, M_V_W), BF16),
            jax.ShapeDtypeStruct((b, s, LANES), F32),
            jax.ShapeDtypeStruct((b, s, LANES), F32),
            jax.ShapeDtypeStruct((b, N_STATE, s), F32),
        ],
        scratch_shapes=[pltpu.VMEM((tm + 2 * CONV_HALO, d), BF16),
                        pltpu.VMEM((tm // sub, sub + 2 * CONV_HALO, qk_w), F32)],
        compiler_params=pltpu.CompilerParams(
            dimension_semantics=("arbitrary", "arbitrary"), vmem_limit_bytes=VMEM_LIMIT),
        name="inproj",
    )(h3, h3, h3, g_pre, w_a, b_gate, *rope_tabs, conv_w, conv_b)


def _attn_kernel(sink_ref, q_ref, kp_ref, kc_ref, kn_ref, vp_ref, vc_ref, vn_ref, o_ref):
    i = pl.program_id(1)
    blk = ATT_BLOCK
    tq = q_ref.shape[0]
    dh = ATT_HEAD_DIM
    group = ATT_HEADS // ATT_KV_HEADS
    rows = group * blk
    n_sub = tq // blk
    qi = lax.broadcasted_iota(jnp.int32, (rows, blk), 0) % blk
    kc = lax.broadcasted_iota(jnp.int32, (rows, blk), 1)
    bias_prev = jnp.where(kc >= qi, 0.0, -jnp.inf)
    bias_next = jnp.where(kc <= qi, 0.0, -jnp.inf)
    first_prev = jnp.where(i == 0, -jnp.inf, bias_prev)
    last_next = jnp.where(i == pl.num_programs(1) - 1, -jnp.inf, bias_next)
    head_of_row = lax.broadcasted_iota(jnp.int32, (rows, 1), 0) // blk
    kcat = jnp.concatenate([kp_ref[...], kc_ref[...], kn_ref[...]], axis=0)
    vcat = jnp.concatenate([vp_ref[...], vc_ref[...], vn_ref[...]], axis=0)
    ones_blk = _ones_column_block(3 * blk, dh)
    sinks = []
    for g in range(ATT_KV_HEADS):
        sink = jnp.zeros((rows, 1), F32)
        for hh in range(group):
            sink = jnp.where(head_of_row == hh, sink_ref[g * group + hh] * LOG2E, sink)
        sinks.append(sink)
    def scores(sb, g):
        qs = jnp.concatenate(
            [q_ref[sb * blk:(sb + 1) * blk, (g * group + hh) * dh:(g * group + hh + 1) * dh]
             for hh in range(group)], axis=0)
        s = _dot_nt(qs, kcat[sb * blk:(sb + 3) * blk, g * dh:(g + 1) * dh])
        bp = first_prev if sb == 0 else bias_prev
        bn = last_next if sb == n_sub - 1 else bias_next
        return jnp.concatenate([s[:, :blk] + bp, s[:, blk:2 * blk], s[:, 2 * blk:] + bn], axis=1)

    for sb0 in range(0, n_sub, ATT_UNIT_SUBS):
        units = [(sb, g) for sb in range(sb0, sb0 + ATT_UNIT_SUBS) for g in range(ATT_KV_HEADS)]
        ss = [scores(sb, g) for sb, g in units]
        mxs = [jnp.maximum(jnp.max(s, axis=-1, keepdims=True), sinks[g]) for s, (sb, g) in zip(ss, units)]
        ps = [jnp.exp2(s - mx).astype(BF16) for s, mx in zip(ss, mxs)]
        pvs = [_dot(p, jnp.concatenate([vcat[sb * blk:(sb + 3) * blk, g * dh:(g + 1) * dh], ones_blk], axis=1))
               for p, (sb, g) in zip(ps, units)]
        os_ = [pv[:, :dh] / (pv[:, dh:dh + 1] + jnp.exp2(sinks[g] - mx))
               for pv, mx, (sb, g) in zip(pvs, mxs, units)]
        for j in range(ATT_UNIT_SUBS):
            outs = [os_[j * ATT_KV_HEADS + g][hh * blk:(hh + 1) * blk]
                    for g in range(ATT_KV_HEADS) for hh in range(group)]
            o_ref[(sb0 + j) * blk:(sb0 + j + 1) * blk, :] = jnp.concatenate(outs, axis=-1).astype(BF16)


def _attention(q, k, v, sink, *, tq=ATT_UNIT_SUBS * ATT_BLOCK):
    b, s, _ = q.shape
    n_tile = s // tq
    bpt = tq // ATT_BLOCK
    n_blk = s // ATT_BLOCK
    cur = lambda bb, i: (bb, i, 0)
    prev = lambda bb, i: (bb, jnp.maximum(i * bpt - 1, 0), 0)
    nxt = lambda bb, i: (bb, jnp.minimum((i + 1) * bpt, n_blk - 1), 0)
    halo_spec = lambda im: pl.BlockSpec((None, ATT_BLOCK, ATT_KV_W), im)
    main_spec = pl.BlockSpec((None, tq, ATT_KV_W), cur)
    return pl.pallas_call(
        _attn_kernel,
        grid=(b, n_tile),
        in_specs=[
            pl.BlockSpec(memory_space=pltpu.SMEM),
            pl.BlockSpec((None, tq, ATT_Q_W), cur),
            halo_spec(prev), main_spec, halo_spec(nxt),
            halo_spec(prev), main_spec, halo_spec(nxt),
        ],
        out_specs=pl.BlockSpec((None, tq, ATT_Q_W), cur),
        out_shape=jax.ShapeDtypeStruct((b, s, ATT_Q_W), BF16),
        compiler_params=pltpu.CompilerParams(
            dimension_semantics=("arbitrary", "arbitrary"), vmem_limit_bytes=VMEM_LIMIT),
        name="attn",
    )(sink, q, k, k, k, v, v, v)


def _mlstm_kernel(qf_ref, ktf_ref, vf_ref, gmf_ref, gbf_ref, grf_ref,
                  qb_ref, ktb_ref, vb_ref, gmb_ref, gbb_ref, grb_ref,
                  of_ref, ob_ref, c_ref, m_ref):
    c = pl.program_id(1)
    L = SCAN_CHUNK
    dk = MLSTM_QK_DIM
    dv = MLSTM_V_DIM

    @pl.when(c == 0)
    def _():
        c_ref[...] = jnp.zeros_like(c_ref)
        m_ref[...] = jnp.zeros_like(m_ref)

    row = lax.broadcasted_iota(jnp.int32, (L, L), 0)
    col = lax.broadcasted_iota(jnp.int32, (L, L), 1)
    dirs = (
        (0, qf_ref, ktf_ref, vf_ref, gmf_ref, gbf_ref, grf_ref, of_ref),
        (1, qb_ref, ktb_ref, vb_ref, gmb_ref, gbb_ref, grb_ref, ob_ref),
    )
    for d, q_ref, kt_ref, v_ref, gm_ref, gb_ref, gr_ref, o_ref in dirs:
        causal = (col <= row) if d == 0 else (col >= row)
        last = L - 1 if d == 0 else 0
        m_prev_v = m_ref[d:d + 1, :]
        m_t = jnp.maximum(gm_ref[...], m_prev_v)
        w_inter_t = jnp.exp(m_prev_v - m_t)
        den_min_t = jnp.exp(-(gb_ref[...] + m_t))
        m_last_v = m_t[last:last + 1, :]
        decay_v = jnp.exp(m_prev_v - m_last_v)
        m_ref[d:d + 1, :] = gb_ref[last:last + 1, :] + m_last_v
        heads = range(MLSTM_HEADS)
        rs = [d * MLSTM_HEADS + h for h in heads]
        qs = [q_ref[:, h * dk:(h + 1) * dk] for h in heads]
        kts = [kt_ref[h * dk:(h + 1) * dk, :] for h in heads]
        v_augs = [jnp.concatenate([v_ref[:, h * dv:(h + 1) * dv], _ones_column_block(L, LANES, rs[h])], axis=1)
                  for h in heads]
        a_rs = [gr_ref[r:r + 1, :] for r in rs]
        c_prevs = [c_ref[r] for r in rs]
        scs = [(_dot(qs[h], kts[h])
                * jnp.exp(jnp.where(causal, a_rs[h] - m_t[:, rs[h]:rs[h] + 1], -jnp.inf))).astype(BF16)
               for h in heads]
        q_inters = [(qs[h].astype(F32) * w_inter_t[:, rs[h]:rs[h] + 1]).astype(BF16) for h in heads]
        tots = [_dot(jnp.concatenate([scs[h], q_inters[h]], axis=1),
                     jnp.concatenate([v_augs[h], c_prevs[h].astype(BF16)], axis=0)) for h in heads]
        for h in heads:
            r = rs[h]
            den = jnp.maximum(jnp.abs(tots[h][:, dv:]), den_min_t)
            o_ref[:, h * dv:(h + 1) * dv] = (tots[h][:, :dv] * (1.0 / den)[:, r:r + 1]).astype(BF16)
        for h in heads:
            r = rs[h]
            w_k = jnp.exp(a_rs[h] - m_last_v[:, r:r + 1])
            kw = (kts[h].astype(F32) * w_k).astype(BF16)
            c_ref[r] = decay_v[:, r:r + 1] * c_prevs[h] + _dot(kw, v_augs[h])


def _mlstm(mq, mkt, mv, gcm, gb, grow):
    b, s, _ = mq.shape
    L = SCAN_CHUNK
    n_chunk = s // L
    fwd = lambda bb, c: (bb, c, 0)
    bwd = lambda bb, c: (bb, n_chunk - 1 - c, 0)
    fwd_t = lambda bb, c: (bb, 0, c)
    bwd_t = lambda bb, c: (bb, 0, n_chunk - 1 - c)

    def dir_specs(tok, tr):
        return [
            pl.BlockSpec((None, L, M_QK_W), tok),
            pl.BlockSpec((None, M_QK_W, L), tr),
            pl.BlockSpec((None, L, M_V_W), tok),
            pl.BlockSpec((None, L, LANES), tok),
            pl.BlockSpec((None, L, LANES), tok),
            pl.BlockSpec((None, N_STATE, L), tr),
        ]

    return pl.pallas_call(
        _mlstm_kernel,
        grid=(b, n_chunk),
        in_specs=dir_specs(fwd, fwd_t) + dir_specs(bwd, bwd_t),
        out_specs=[pl.BlockSpec((None, L, M_V_W), fwd), pl.BlockSpec((None, L, M_V_W), bwd)],
        out_shape=[jax.ShapeDtypeStruct((b, s, M_V_W), BF16)] * 2,
        scratch_shapes=[
            pltpu.VMEM((N_STATE, MLSTM_QK_DIM, V_AUG), F32),
            pltpu.VMEM((SUBLANES, LANES), F32),
        ],
        compiler_params=pltpu.CompilerParams(
            dimension_semantics=("arbitrary", "arbitrary"), vmem_limit_bytes=VMEM_LIMIT),
        name="mlstm",
    )(mq, mkt, mv, gcm, gb, grow, mq, mkt, mv, gcm, gb, grow)


def _merge_kernel(sub, h_ref, oa_ref, hf_ref, hb_ref, gpre_ref, wmo_ref, wbg_ref, wba_ref, wbm_ref, wout_ref,
                  gm_ref, gpost_ref, o_ref):
    dv = MLSTM_V_DIM
    n_sub = h_ref.shape[0] // sub
    rows = [slice(s * sub, (s + 1) * sub) for s in range(n_sub)]

    def prologue(s):
        u = _rms(h_ref[rows[s], :], gpre_ref[...]).astype(BF16)
        hm = hf_ref[rows[s], :].astype(F32) + hb_ref[rows[s], :].astype(F32)
        hm = jnp.concatenate(
            [hm[:, i * dv:(i + 1) * dv]
             * lax.rsqrt(jnp.mean(hm[:, i * dv:(i + 1) * dv] ** 2, axis=-1, keepdims=True) + RMS_EPS)
             for i in range(MLSTM_HEADS)], axis=-1)
        return u, hm * gm_ref[...]

    subs = range(n_sub)
    pre = [prologue(s) for s in subs]
    us = [p[0] for p in pre]
    hms = [(pre[s][1] * _sigmoid(_dot(us[s], wmo_ref[...]))).astype(BF16) for s in subs]
    yms = [_dot(hms[s], wbm_ref[...]) for s in subs]
    yas = [_dot(oa_ref[rows[s], :], wba_ref[...]) for s in subs]
    merged = [(_sigmoid(_dot(us[s], wbg_ref[:, :D_MODEL])) * yas[s]
               + _sigmoid(_dot(us[s], wbg_ref[:, D_MODEL:])) * yms[s]).astype(BF16) for s in subs]
    mixes = [_dot(merged[s], wout_ref[...]) for s in subs]
    for s in subs:
        o_ref[rows[s], :] = h_ref[rows[s], :] + _rms(mixes[s], gpost_ref[...])


def _merge(h, oa, hf, hb, g_pre, w_mo, w_bg, w_ba, w_bm, w_out, g_m, g_post, *, tm=1024, sub=256):
    t, d = h.shape
    row = lambda i: (i, 0)
    full = lambda a: _resident(a.shape)
    return pl.pallas_call(
        functools.partial(_merge_kernel, sub),
        grid=(t // tm,),
        in_specs=[
            pl.BlockSpec((tm, d), row), pl.BlockSpec((tm, ATT_Q_W), row),
            pl.BlockSpec((tm, M_V_W), row), pl.BlockSpec((tm, M_V_W), row),
            full(g_pre), full(w_mo), full(w_bg), full(w_ba), full(w_bm), full(w_out), full(g_m), full(g_post),
        ],
        out_specs=pl.BlockSpec((tm, d), row),
        out_shape=jax.ShapeDtypeStruct((t, d), F32),
        compiler_params=pltpu.CompilerParams(
            dimension_semantics=("arbitrary",), vmem_limit_bytes=VMEM_LIMIT),
        name="merge",
    )(h, oa, hf, hb, g_pre, w_mo, w_bg, w_ba, w_bm, w_out, g_m, g_post)


def _rope_tables(seq):
    pos = jnp.arange(seq, dtype=F32)
    inv = ROPE_THETA ** (-jnp.arange(0, ROPE_DIM, 2, dtype=F32) / ROPE_DIM)
    ang = pos[:, None] * inv[None, :]
    cos, sin = jnp.cos(ang), jnp.sin(ang)
    half = ROPE_DIM // 2
    dpos = jnp.arange(LANES) % ATT_HEAD_DIM
    sel = dpos % half
    rc = jnp.where(dpos < ROPE_DIM, cos[:, sel], 1.0)
    rs1 = jnp.where(dpos < half, -sin[:, sel], 0.0)
    rs2 = jnp.where((dpos >= half) & (dpos < ROPE_DIM), sin[:, sel], 0.0)
    return rc, rs1, rs2


def _gate_columns(a):
    g = a.reshape(a.shape[:-1] + (4, MLSTM_HEADS))
    g = jnp.stack([g[..., 0, :], g[..., 2, :], g[..., 1, :], g[..., 3, :]], axis=-2)
    g = g.reshape(a.shape[:-1] + (M_GATE_W,))
    return jnp.pad(g, [(0, 0)] * (a.ndim - 1) + [(0, LANES - M_GATE_W)])


def kernel(x, p, ffn1_norm_pre, ffn1_w1, ffn1_w2, ffn1_norm_post, mix_norm_pre, w_in, b_gates, conv_w, conv_b,
           attn_sink, mlstm_norm, w_branch_attn, w_branch_mlstm, w_out, mix_norm_post, ffn2_norm_pre, ffn2_w1,
           ffn2_w2, ffn2_norm_post, ple_norm_pre, w_ple_gate, w_ple_proj, ple_norm_post):
    bsz, seq, d = x.shape
    depth = p.shape[0]
    h = x.reshape(bsz * seq, d)
    rope_tabs = _rope_tables(seq)
    bf = lambda a: a.astype(BF16)
    for i in range(depth):
        h = _ffn(h, ffn1_norm_pre[i][None], bf(ffn1_w1[i]), bf(ffn1_w2[i]), ffn1_norm_post[i][None], name="ffn1")
        wi = w_in[i]
        w_a = bf(jnp.concatenate([wi[:, :OFF_MO], _gate_columns(wi[:, OFF_MG:OFF_BG])], axis=1))
        b_gate = _gate_columns(b_gates[i])[None]
        q, k, v, mq, mkt, mv, gcm, gb, grow = _inproj(
            h.reshape(bsz, seq, d), mix_norm_pre[i][None], w_a, b_gate, rope_tabs, conv_w[i], conv_b[i][None])
        oa = _attention(q, k, v, attn_sink[i])
        hf, hb = _mlstm(mq, mkt, mv, gcm, gb, grow)
        h = _merge(h, oa.reshape(bsz * seq, ATT_Q_W), hf.reshape(bsz * seq, M_V_W), hb.reshape(bsz * seq, M_V_W),
                   mix_norm_pre[i][None], bf(wi[:, OFF_MO:OFF_MG]), bf(wi[:, OFF_BG:]), bf(w_branch_attn[i]),
                   bf(w_branch_mlstm[i]), bf(w_out[i]), mlstm_norm[i][None], mix_norm_post[i][None])
        h = _ffn(h, ffn2_norm_pre[i][None], bf(ffn2_w1[i]), bf(ffn2_w2[i]), ffn2_norm_post[i][None],
                 ple=(p.reshape(depth * bsz * seq, PLE_DIM), i * bsz * seq, ple_norm_pre[i][None], bf(w_ple_gate[i]),
                      bf(w_ple_proj[i]), ple_norm_post[i][None]), name="ffn2_ple")
    return h.reshape(bsz, seq, d)
```

```python
import functools

import jax
import jax.numpy as jnp
from jax import lax
from jax.experimental import pallas as pl
from jax.experimental.pallas import tpu as pltpu

D_MODEL = 1024
PLE_DIM = 256
RMS_EPS = 1e-6
MACARON = 0.5
FFN_HIDDEN = 2816
ATT_HEADS = 8
ATT_KV_HEADS = 2
ATT_HEAD_DIM = 64
ATT_WINDOW = 128
ATT_BLOCK = 128
ROPE_THETA = 500000.0
ROPE_DIM = ATT_HEAD_DIM // 4
MLSTM_HEADS = 4
MLSTM_QK_DIM = 128
MLSTM_V_DIM = 256
MLSTM_CONV = 5
ATT_Q_W = ATT_HEADS * ATT_HEAD_DIM
ATT_KV_W = ATT_KV_HEADS * ATT_HEAD_DIM
M_QK_W = MLSTM_HEADS * MLSTM_QK_DIM
M_V_W = MLSTM_HEADS * MLSTM_V_DIM
M_GATE_W = 4 * MLSTM_HEADS

OFF_AQ = 0
OFF_AK = OFF_AQ + ATT_Q_W
OFF_AV = OFF_AK + ATT_KV_W
OFF_MQ = OFF_AV + ATT_KV_W
OFF_MV = OFF_MQ + 2 * M_QK_W
OFF_MO = OFF_MV + M_V_W
OFF_MG = OFF_MO + M_V_W
OFF_BG = OFF_MG + M_GATE_W
IN_WIDTH = OFF_BG + 2 * D_MODEL

LANES = 128
SUBLANES = 8
BF16_ROWS = 16
VMEM_LIMIT = 56 * 1024 * 1024

SCAN_CHUNK = 256
N_STATE = 2 * MLSTM_HEADS
V_AUG = MLSTM_V_DIM + LANES
CONV_HALO = BF16_ROWS

LOG2E = 1.4426950408889634
ATT_UNIT_SUBS = 8

BF16 = jnp.bfloat16
F32 = jnp.float32


def _rms(x, g):
    return x * lax.rsqrt(jnp.mean(x * x, axis=-1, keepdims=True) + RMS_EPS) * g


def _sigmoid(x):
    return 1.0 / (1.0 + jnp.exp(-x))


def _dot(a, b):
    return jnp.dot(a, b, preferred_element_type=F32)


def _dot_nt(a, b):
    return lax.dot_general(a, b, (((1,), (1,)), ((), ())), preferred_element_type=F32)


def _ones_column_block(rows, width, col=0):
    return jnp.where(lax.broadcasted_iota(jnp.int32, (rows, width), 1) == col, 1.0, 0.0).astype(BF16)


def _ffn_kernel(sub, hid_chunk, with_ple, cast_cols, *refs):
    n_main = 10 if with_ple else 5
    n_cast = len(cast_cols)
    if with_ple:
        (h_ref, gpre_ref, w1_ref, w2_ref, gpost_ref, p_ref, gple_pre_ref, wpg_ref, wpp_ref,
         gple_post_ref) = refs[:n_main]
    else:
        h_ref, gpre_ref, w1_ref, w2_ref, gpost_ref = refs[:n_main]
    cast_in = refs[n_main:n_main + n_cast]
    o_ref = refs[n_main + n_cast]
    cast_out = iter(refs[n_main + n_cast + 1:])
    for src_ref, cols in zip(cast_in, cast_cols):
        for c0, c1 in cols:
            next(cast_out)[...] = src_ref[:, c0:c1].astype(BF16)

    n_sub = h_ref.shape[0] // sub
    rows = [slice(s * sub, (s + 1) * sub) for s in range(n_sub)]
    xn = [_rms(h_ref[r, :], gpre_ref[...]).astype(BF16) for r in rows]

    def main(s):
        acc = None
        for c in range(0, FFN_HIDDEN, hid_chunk):
            z1 = _dot(xn[s], w1_ref[:, c:c + hid_chunk])
            z2 = _dot(xn[s], w1_ref[:, FFN_HIDDEN + c:FFN_HIDDEN + c + hid_chunk])
            part = _dot((z1 * _sigmoid(z1) * z2).astype(BF16), w2_ref[c:c + hid_chunk, :])
            acc = part if acc is None else acc + part
        return acc

    def epilogue(s, acc):
        h = h_ref[rows[s], :] + MACARON * _rms(acc, gpost_ref[...])
        if with_ple:
            gate = _sigmoid(_dot(_rms(h, gple_pre_ref[...]).astype(BF16), wpg_ref[...]))
            proj = _dot(p_ref[rows[s], :].astype(BF16), wpp_ref[...])
            h = h + _rms(proj * gate, gple_post_ref[...])
        o_ref[rows[s], :] = h

    prev = main(0)
    for s in range(1, n_sub):
        cur = main(s)
        epilogue(s - 1, prev)
        prev = cur
    epilogue(n_sub - 1, prev)


def _resident(shape):
    return pl.BlockSpec(shape, lambda *_: (0,) * len(shape), pipeline_mode=pl.Buffered(1))


def _ffn(h, g_pre, w1, w2, g_post, ple=None, casts=(), *, tm=1024, sub=256, hid_chunk=2816, name="ffn"):
    t, d = h.shape
    n_step = t // tm
    row = lambda i: (i, 0)
    in_specs = [pl.BlockSpec((tm, d), row), _resident((1, d)), _resident(w1.shape), _resident(w2.shape),
                _resident((1, d))]
    args = [h, g_pre, w1, w2, g_post]
    if ple is not None:
        p, p_row0, g_ple_pre, w_gate, w_proj, g_ple_post = ple
        in_specs += [pl.BlockSpec((tm, PLE_DIM), lambda i: (i + p_row0 // tm, 0)), _resident((1, d)),
                     _resident(w_gate.shape),
                     _resident(w_proj.shape), _resident((1, d))]
        args += [p, g_ple_pre, w_gate, w_proj, g_ple_post]
    out_specs = [pl.BlockSpec((tm, d), row)]
    out_shape = [jax.ShapeDtypeStruct((t, d), F32)]
    for src, cols in casts:
        n_rows = src.shape[0]
        n_blk = max(nb for nb in range(1, n_step + 1)
                    if n_step % nb == 0 and n_rows % (nb * BF16_ROWS) == 0)
        blk_map = functools.partial(lambda spb, i: (i // spb, 0), n_step // n_blk)
        in_specs.append(pl.BlockSpec((n_rows // n_blk, src.shape[1]), blk_map))
        args.append(src)
        for c0, c1 in cols:
            out_specs.append(pl.BlockSpec((n_rows // n_blk, c1 - c0), blk_map))
            out_shape.append(jax.ShapeDtypeStruct((n_rows, c1 - c0), BF16))
    outs = pl.pallas_call(
        functools.partial(_ffn_kernel, sub, hid_chunk, ple is not None, tuple(tuple(c) for _, c in casts)),
        grid=(n_step,),
        in_specs=in_specs,
        out_specs=out_specs,
        out_shape=out_shape,
        compiler_params=pltpu.CompilerParams(dimension_semantics=("arbitrary",), vmem_limit_bytes=VMEM_LIMIT),
        name=name,
    )(*args)
    return outs[0], list(outs[1:])


WA_Q = 0
WA_KV = WA_Q + ATT_Q_W
WA_MQK = WA_KV + 2 * ATT_KV_W
WA_MV = WA_MQK + 2 * M_QK_W
WA_END = WA_MV + M_V_W


def _lane_scan(x, op, ident, reverse):
    width = x.shape[1]
    lane = lax.broadcasted_iota(jnp.int32, x.shape, 1)
    s = 1
    while s < width:
        if reverse:
            shifted = jnp.where(lane < width - s, pltpu.roll(x, width - s, 1), ident)
        else:
            shifted = jnp.where(lane >= s, pltpu.roll(x, s, 1), ident)
        x = op(x, shifted)
        s *= 2
    return x


def _inproj_kernel(n_tile, sub, h_ref, hp_ref, hn_ref, g_ref, w_ref, wg_ref, bg_ref, rc_ref, rs1_ref, rs2_ref, cw_ref,
                   cb_ref, q_ref, k_ref, v_ref, mq_ref, mkt_ref, mv_ref, gcm_ref, gb_ref, grow_ref, ue_ref, ze_ref):
    i = pl.program_id(0)
    tm = h_ref.shape[0]
    g_pre = g_ref[...]
    halo = CONV_HALO
    ue_ref[0:halo, :] = jnp.where(i > 0, _rms(hp_ref[...], g_pre), 0.0).astype(BF16)
    ue_ref[halo:halo + tm, :] = _rms(h_ref[...], g_pre).astype(BF16)
    ue_ref[halo + tm:, :] = jnp.where(i < n_tile - 1, _rms(hn_ref[...], g_pre), 0.0).astype(BF16)
    is_fwd = lax.broadcasted_iota(jnp.int32, (N_STATE, SCAN_CHUNK), 0) < MLSTM_HEADS
    cat = lambda parts: jnp.concatenate(parts, axis=1)

    n_sub = tm // sub
    rows = [slice(s * sub, (s + 1) * sub) for s in range(n_sub)]
    u_of = lambda s: ue_ref[halo + s * sub:halo + (s + 1) * sub, :]

    def attn_proj(s):
        rc, rs1, rs2 = rc_ref[rows[s], :], rs1_ref[rows[s], :], rs2_ref[rows[s], :]

        def rope(x):
            half = ROPE_DIM // 2
            return x * rc + pltpu.roll(x, LANES - half, 1) * rs1 + pltpu.roll(x, half, 1) * rs2

        zq = _dot(u_of(s), w_ref[:, WA_Q:WA_KV])
        for c in range(ATT_Q_W // LANES):
            q_ref[rows[s], c * LANES:(c + 1) * LANES] = (
                rope(zq[:, c * LANES:(c + 1) * LANES]) * (ATT_HEAD_DIM ** -0.5 * LOG2E)).astype(BF16)
        zkv = _dot(u_of(s), w_ref[:, WA_KV:WA_MQK])
        k_ref[rows[s], :] = rope(zkv[:, :ATT_KV_W]).astype(BF16)
        v_ref[rows[s], :] = zkv[:, ATT_KV_W:].astype(BF16)

    def value_proj(s):
        mv_ref[rows[s], :] = _dot(u_of(s), w_ref[:, WA_MV:WA_END]).astype(BF16)

    def qk_proj(s):
        ze_ref[s] = _dot(ue_ref[s * sub:(s + 1) * sub + 2 * halo, :], w_ref[:, WA_MQK:WA_MV])

    def qk_conv(s):
        ze = ze_ref.at[s]
        cw = cw_ref[...]
        y = cb_ref[...]
        for j in range(MLSTM_CONV):
            off = halo - MLSTM_CONV // 2 + j
            y = y + ze[off:off + sub, :] * cw[j:j + 1, :]
        qk = y * _sigmoid(y)
        mq_ref[rows[s], :] = qk[:, :M_QK_W].astype(BF16)
        mkt_ref[:, rows[s]] = (qk[:, M_QK_W:] * (MLSTM_QK_DIM ** -0.5)).T.astype(BF16)

    for s in range(n_sub):
        attn_proj(s)
        value_proj(s)
        qk_proj(s)
        qk_conv(s)
        u = u_of(s)
        rows_s = rows[s]
        gt = (_dot(u, wg_ref[...]) + bg_ref[...]).T
        li = gt[0:N_STATE]
        f_raw = gt[N_STATE:2 * N_STATE]
        lf = jnp.minimum(f_raw, 0.0) - jnp.log1p(jnp.exp(-jnp.abs(f_raw)))
        a_parts, cm_parts, b_parts = [], [], []
        for c in range(sub // SCAN_CHUNK):
            sl = slice(c * SCAN_CHUNK, (c + 1) * SCAN_CHUNK)
            lfc = lf[:, sl]
            b = jnp.where(is_fwd, _lane_scan(lfc, jnp.add, 0.0, False), _lane_scan(lfc, jnp.add, 0.0, True))
            a = li[:, sl] - b
            cm = jnp.where(is_fwd, _lane_scan(a, jnp.maximum, -jnp.inf, False),
                           _lane_scan(a, jnp.maximum, -jnp.inf, True))
            a_parts.append(a)
            cm_parts.append(cm)
            b_parts.append(b)
        grow_ref[:, rows_s] = cat(a_parts)
        pad = jnp.zeros((LANES - N_STATE, sub), F32)
        gcm_ref[rows_s, :] = jnp.concatenate([cat(cm_parts), pad], axis=0).T
        gb_ref[rows_s, :] = jnp.concatenate([cat(b_parts), pad], axis=0).T


def _inproj(h3, g_pre, w_a, w_g, b_gate, rope_tabs, conv_w, conv_b, *, tm=1024, sub=512):
    b, s, d = h3.shape
    n_tile = s // tm
    hpt = tm // CONV_HALO
    n_halo = s // CONV_HALO
    tok = lambda i, bb: (bb, i, 0)
    tok_p = lambda i, bb: (bb, jnp.maximum(i * hpt - 1, 0), 0)
    tok_n = lambda i, bb: (bb, jnp.minimum((i + 1) * hpt, n_halo - 1), 0)
    tab = lambda i, bb: (i, 0)
    qk_w = 2 * M_QK_W
    tok_spec = lambda w: pl.BlockSpec((None, tm, w), tok)
    tr_spec = lambda r: pl.BlockSpec((None, r, tm), lambda i, bb: (bb, 0, i))
    return pl.pallas_call(
        functools.partial(_inproj_kernel, n_tile, sub),
        grid=(n_tile, b),
        in_specs=[
            tok_spec(d),
            pl.BlockSpec((None, CONV_HALO, d), tok_p),
            pl.BlockSpec((None, CONV_HALO, d), tok_n),
            _resident((1, d)),
            _resident((d, WA_END)),
            _resident((d, LANES)),
            _resident((1, LANES)),
            pl.BlockSpec((tm, LANES), tab),
            pl.BlockSpec((tm, LANES), tab),
            pl.BlockSpec((tm, LANES), tab),
            _resident((MLSTM_CONV, qk_w)),
            _resident((1, qk_w)),
        ],
        out_specs=[tok_spec(ATT_Q_W), tok_spec(ATT_KV_W), tok_spec(ATT_KV_W), tok_spec(M_QK_W), tr_spec(M_QK_W),
                   tok_spec(M_V_W), tok_spec(LANES), tok_spec(LANES), tr_spec(N_STATE)],
        out_shape=[
            jax.ShapeDtypeStruct((b, s, ATT_Q_W), BF16),
            jax.ShapeDtypeStruct((b, s, ATT_KV_W), BF16),
            jax.ShapeDtypeStruct((b, s, ATT_KV_W), BF16),
            jax.ShapeDtypeStruct((b, s, M_QK_W), BF16),
            jax.ShapeDtypeStruct((b, M_QK_W, s), BF16),
            jax.ShapeDtypeStruct((b, s, M_V_W), BF16),
            jax.ShapeDtypeStruct((b, s, LANES), F32),
            jax.ShapeDtypeStruct((b, s, LANES), F32),
            jax.ShapeDtypeStruct((b, N_STATE, s), F32),
        ],
        scratch_shapes=[pltpu.VMEM((tm + 2 * CONV_HALO, d), BF16),
                        pltpu.VMEM((tm // sub, sub + 2 * CONV_HALO, qk_w), F32)],
        compiler_params=pltpu.CompilerParams(
            dimension_semantics=("arbitrary", "arbitrary"), vmem_limit_bytes=VMEM_LIMIT),
        name="inproj",
    )(h3, h3, h3, g_pre, w_a, w_g, b_gate, *rope_tabs, conv_w, conv_b)


def _attn_kernel(sink_ref, q_ref, kp_ref, kc_ref, kn_ref, vp_ref, vc_ref, vn_ref, o_ref):
    i = pl.program_id(1)
    blk = ATT_BLOCK
    tq = q_ref.shape[0]
    dh = ATT_HEAD_DIM
    group = ATT_HEADS // ATT_KV_HEADS
    rows = group * blk
    n_sub = tq // blk
    qi = lax.broadcasted_iota(jnp.int32, (rows, blk), 0) % blk
    kc = lax.broadcasted_iota(jnp.int32, (rows, blk), 1)
    bias_prev = jnp.where(kc >= qi, 0.0, -jnp.inf)
    bias_next = jnp.where(kc <= qi, 0.0, -jnp.inf)
    first_prev = jnp.where(i == 0, -jnp.inf, bias_prev)
    last_next = jnp.where(i == pl.num_programs(1) - 1, -jnp.inf, bias_next)
    head_of_row = lax.broadcasted_iota(jnp.int32, (rows, 1), 0) // blk
    kcat = jnp.concatenate([kp_ref[...], kc_ref[...], kn_ref[...]], axis=0)
    vcat = jnp.concatenate([vp_ref[...], vc_ref[...], vn_ref[...]], axis=0)
    ones_blk = _ones_column_block(3 * blk, dh)
    sinks = []
    for g in range(ATT_KV_HEADS):
        sink = jnp.zeros((rows, 1), F32)
        for hh in range(group):
            sink = jnp.where(head_of_row == hh, sink_ref[g * group + hh] * LOG2E, sink)
        sinks.append(sink)
    def scores(sb, g):
        qs = jnp.concatenate(
            [q_ref[sb * blk:(sb + 1) * blk, (g * group + hh) * dh:(g * group + hh + 1) * dh]
             for hh in range(group)], axis=0)
        s = _dot_nt(qs, kcat[sb * blk:(sb + 3) * blk, g * dh:(g + 1) * dh])
        bp = first_prev if sb == 0 else bias_prev
        bn = last_next if sb == n_sub - 1 else bias_next
        return jnp.concatenate([s[:, :blk] + bp, s[:, blk:2 * blk], s[:, 2 * blk:] + bn], axis=1)

    for sb0 in range(0, n_sub, ATT_UNIT_SUBS):
        units = [(sb, g) for sb in range(sb0, sb0 + ATT_UNIT_SUBS) for g in range(ATT_KV_HEADS)]
        ss = [scores(sb, g) for sb, g in units]
        mxs = [jnp.maximum(jnp.max(s, axis=-1, keepdims=True), sinks[g]) for s, (sb, g) in zip(ss, units)]
        ps = [jnp.exp2(s - mx).astype(BF16) for s, mx in zip(ss, mxs)]
        pvs = [_dot(p, jnp.concatenate([vcat[sb * blk:(sb + 3) * blk, g * dh:(g + 1) * dh], ones_blk], axis=1))
               for p, (sb, g) in zip(ps, units)]
        os_ = [pv[:, :dh] / (pv[:, dh:dh + 1] + jnp.exp2(sinks[g] - mx))
               for pv, mx, (sb, g) in zip(pvs, mxs, units)]
        for j in range(ATT_UNIT_SUBS):
            outs = [os_[j * ATT_KV_HEADS + g][hh * blk:(hh + 1) * blk]
                    for g in range(ATT_KV_HEADS) for hh in range(group)]
            o_ref[(sb0 + j) * blk:(sb0 + j + 1) * blk, :] = jnp.concatenate(outs, axis=-1).astype(BF16)


def _attention(q, k, v, sink, *, tq=ATT_UNIT_SUBS * ATT_BLOCK):
    b, s, _ = q.shape
    n_tile = s // tq
    bpt = tq // ATT_BLOCK
    n_blk = s // ATT_BLOCK
    cur = lambda bb, i: (bb, i, 0)
    prev = lambda bb, i: (bb, jnp.maximum(i * bpt - 1, 0), 0)
    nxt = lambda bb, i: (bb, jnp.minimum((i + 1) * bpt, n_blk - 1), 0)
    halo_spec = lambda im: pl.BlockSpec((None, ATT_BLOCK, ATT_KV_W), im)
    main_spec = pl.BlockSpec((None, tq, ATT_KV_W), cur)
    return pl.pallas_call(
        _attn_kernel,
        grid=(b, n_tile),
        in_specs=[
            pl.BlockSpec(memory_space=pltpu.SMEM),
            pl.BlockSpec((None, tq, ATT_Q_W), cur),
            halo_spec(prev), main_spec, halo_spec(nxt),
            halo_spec(prev), main_spec, halo_spec(nxt),
        ],
        out_specs=pl.BlockSpec((None, tq, ATT_Q_W), cur),
        out_shape=jax.ShapeDtypeStruct((b, s, ATT_Q_W), BF16),
        compiler_params=pltpu.CompilerParams(
            dimension_semantics=("arbitrary", "arbitrary"), vmem_limit_bytes=VMEM_LIMIT),
        name="attn",
    )(sink, q, k, k, k, v, v, v)


def _mlstm_kernel(qf_ref, ktf_ref, vf_ref, gmf_ref, gbf_ref, grf_ref,
                  qb_ref, ktb_ref, vb_ref, gmb_ref, gbb_ref, grb_ref,
                  of_ref, ob_ref, c_ref, m_ref):
    c = pl.program_id(1)
    L = SCAN_CHUNK
    dk = MLSTM_QK_DIM
    dv = MLSTM_V_DIM

    @pl.when(c == 0)
    def _():
        c_ref[...] = jnp.zeros_like(c_ref)
        m_ref[...] = jnp.zeros_like(m_ref)

    row = lax.broadcasted_iota(jnp.int32, (L, L), 0)
    col = lax.broadcasted_iota(jnp.int32, (L, L), 1)
    dirs = (
        (0, qf_ref, ktf_ref, vf_ref, gmf_ref, gbf_ref, grf_ref, of_ref),
        (1, qb_ref, ktb_ref, vb_ref, gmb_ref, gbb_ref, grb_ref, ob_ref),
    )
    for d, q_ref, kt_ref, v_ref, gm_ref, gb_ref, gr_ref, o_ref in dirs:
        causal = (col <= row) if d == 0 else (col >= row)
        last = L - 1 if d == 0 else 0
        m_prev_v = m_ref[d:d + 1, :]
        m_t = jnp.maximum(gm_ref[...], m_prev_v)
        w_inter_t = jnp.exp(m_prev_v - m_t)
        den_min_t = jnp.exp(-(gb_ref[...] + m_t))
        m_last_v = m_t[last:last + 1, :]
        decay_v = jnp.exp(m_prev_v - m_last_v)
        m_ref[d:d + 1, :] = gb_ref[last:last + 1, :] + m_last_v
        heads = range(MLSTM_HEADS)
        rs = [d * MLSTM_HEADS + h for h in heads]
        qs = [q_ref[:, h * dk:(h + 1) * dk] for h in heads]
        kts = [kt_ref[h * dk:(h + 1) * dk, :] for h in heads]
        v_augs = [jnp.concatenate([v_ref[:, h * dv:(h + 1) * dv], _ones_column_block(L, LANES, rs[h])], axis=1)
                  for h in heads]
        a_rs = [gr_ref[r:r + 1, :] for r in rs]
        c_prevs = [c_ref[r] for r in rs]
        scs = [(_dot(qs[h], kts[h])
                * jnp.exp(jnp.where(causal, a_rs[h] - m_t[:, rs[h]:rs[h] + 1], -jnp.inf))).astype(BF16)
               for h in heads]
        q_inters = [(qs[h].astype(F32) * w_inter_t[:, rs[h]:rs[h] + 1]).astype(BF16) for h in heads]
        tots = [_dot(jnp.concatenate([scs[h], q_inters[h]], axis=1),
                     jnp.concatenate([v_augs[h], c_prevs[h].astype(BF16)], axis=0)) for h in heads]
        for h in heads:
            r = rs[h]
            den = jnp.maximum(jnp.abs(tots[h][:, dv:]), den_min_t)
            o_ref[:, h * dv:(h + 1) * dv] = (tots[h][:, :dv] * (1.0 / den)[:, r:r + 1]).astype(BF16)
        for h in heads:
            r = rs[h]
            w_k = jnp.exp(a_rs[h] - m_last_v[:, r:r + 1])
            kw = (kts[h].astype(F32) * w_k).astype(BF16)
            c_ref[r] = decay_v[:, r:r + 1] * c_prevs[h] + _dot(kw, v_augs[h])


def _mlstm(mq, mkt, mv, gcm, gb, grow):
    b, s, _ = mq.shape
    L = SCAN_CHUNK
    n_chunk = s // L
    fwd = lambda bb, c: (bb, c, 0)
    bwd = lambda bb, c: (bb, n_chunk - 1 - c, 0)
    fwd_t = lambda bb, c: (bb, 0, c)
    bwd_t = lambda bb, c: (bb, 0, n_chunk - 1 - c)

    def dir_specs(tok, tr):
        return [
            pl.BlockSpec((None, L, M_QK_W), tok),
            pl.BlockSpec((None, M_QK_W, L), tr),
            pl.BlockSpec((None, L, M_V_W), tok),
            pl.BlockSpec((None, L, LANES), tok),
            pl.BlockSpec((None, L, LANES), tok),
            pl.BlockSpec((None, N_STATE, L), tr),
        ]

    return pl.pallas_call(
        _mlstm_kernel,
        grid=(b, n_chunk),
        in_specs=dir_specs(fwd, fwd_t) + dir_specs(bwd, bwd_t),
        out_specs=[pl.BlockSpec((None, L, M_V_W), fwd), pl.BlockSpec((None, L, M_V_W), bwd)],
        out_shape=[jax.ShapeDtypeStruct((b, s, M_V_W), BF16)] * 2,
        scratch_shapes=[
            pltpu.VMEM((N_STATE, MLSTM_QK_DIM, V_AUG), F32),
            pltpu.VMEM((SUBLANES, LANES), F32),
        ],
        compiler_params=pltpu.CompilerParams(
            dimension_semantics=("arbitrary", "arbitrary"), vmem_limit_bytes=VMEM_LIMIT),
        name="mlstm",
    )(mq, mkt, mv, gcm, gb, grow, mq, mkt, mv, gcm, gb, grow)


def _merge_kernel(sub, h_ref, oa_ref, hf_ref, hb_ref, gpre_ref, wmo_ref, wbg_ref, wba_ref, wbm_ref, wout_ref,
                  gm_ref, gpost_ref, o_ref):
    dv = MLSTM_V_DIM
    n_sub = h_ref.shape[0] // sub
    rows = [slice(s * sub, (s + 1) * sub) for s in range(n_sub)]

    def prologue(s):
        u = _rms(h_ref[rows[s], :], gpre_ref[...]).astype(BF16)
        hm = hf_ref[rows[s], :].astype(F32) + hb_ref[rows[s], :].astype(F32)
        hm = jnp.concatenate(
            [hm[:, i * dv:(i + 1) * dv]
             * lax.rsqrt(jnp.mean(hm[:, i * dv:(i + 1) * dv] ** 2, axis=-1, keepdims=True) + RMS_EPS)
             for i in range(MLSTM_HEADS)], axis=-1)
        return u, hm * gm_ref[...]

    subs = range(n_sub)
    pre = [prologue(s) for s in subs]
    us = [p[0] for p in pre]
    hms = [(pre[s][1] * _sigmoid(_dot(us[s], wmo_ref[...]))).astype(BF16) for s in subs]
    yms = [_dot(hms[s], wbm_ref[...]) for s in subs]
    yas = [_dot(oa_ref[rows[s], :], wba_ref[...]) for s in subs]
    merged = [(_sigmoid(_dot(us[s], wbg_ref[:, :D_MODEL])) * yas[s]
               + _sigmoid(_dot(us[s], wbg_ref[:, D_MODEL:])) * yms[s]).astype(BF16) for s in subs]
    mixes = [_dot(merged[s], wout_ref[...]) for s in subs]
    for s in subs:
        o_ref[rows[s], :] = h_ref[rows[s], :] + _rms(mixes[s], gpost_ref[...])


def _merge(h, oa, hf, hb, g_pre, w_mo, w_bg, w_ba, w_bm, w_out, g_m, g_post, *, tm=1024, sub=256):
    t, d = h.shape
    row = lambda i: (i, 0)
    full = lambda a: _resident(a.shape)
    return pl.pallas_call(
        functools.partial(_merge_kernel, sub),
        grid=(t // tm,),
        in_specs=[
            pl.BlockSpec((tm, d), row), pl.BlockSpec((tm, ATT_Q_W), row),
            pl.BlockSpec((tm, M_V_W), row), pl.BlockSpec((tm, M_V_W), row),
            full(g_pre), full(w_mo), full(w_bg), full(w_ba), full(w_bm), full(w_out), full(g_m), full(g_post),
        ],
        out_specs=pl.BlockSpec((tm, d), row),
        out_shape=jax.ShapeDtypeStruct((t, d), F32),
        compiler_params=pltpu.CompilerParams(
            dimension_semantics=("arbitrary",), vmem_limit_bytes=VMEM_LIMIT),
        name="merge",
    )(h, oa, hf, hb, g_pre, w_mo, w_bg, w_ba, w_bm, w_out, g_m, g_post)


def _rope_tables(seq):
    pos = jnp.arange(seq, dtype=F32)
    inv = ROPE_THETA ** (-jnp.arange(0, ROPE_DIM, 2, dtype=F32) / ROPE_DIM)
    ang = pos[:, None] * inv[None, :]
    cos, sin = jnp.cos(ang), jnp.sin(ang)
    half = ROPE_DIM // 2
    dpos = jnp.arange(LANES) % ATT_HEAD_DIM
    sel = dpos % half
    rc = jnp.where(dpos < ROPE_DIM, cos[:, sel], 1.0)
    rs1 = jnp.where(dpos < half, -sin[:, sel], 0.0)
    rs2 = jnp.where((dpos >= half) & (dpos < ROPE_DIM), sin[:, sel], 0.0)
    return rc, rs1, rs2


def _gate_columns(a):
    g = a.reshape(a.shape[:-1] + (4, MLSTM_HEADS))
    g = jnp.stack([g[..., 0, :], g[..., 2, :], g[..., 1, :], g[..., 3, :]], axis=-2)
    g = g.reshape(a.shape[:-1] + (M_GATE_W,))
    return jnp.pad(g, [(0, 0)] * (a.ndim - 1) + [(0, LANES - M_GATE_W)])


def kernel(x, p, ffn1_norm_pre, ffn1_w1, ffn1_w2, ffn1_norm_post, mix_norm_pre, w_in, b_gates, conv_w, conv_b,
           attn_sink, mlstm_norm, w_branch_attn, w_branch_mlstm, w_out, mix_norm_post, ffn2_norm_pre, ffn2_w1,
           ffn2_w2, ffn2_norm_post, ple_norm_pre, w_ple_gate, w_ple_proj, ple_norm_post):
    bsz, seq, d = x.shape
    depth = p.shape[0]
    h = x.reshape(bsz * seq, d)
    rope_tabs = _rope_tables(seq)
    bf = lambda a: a.astype(BF16)
    whole = lambda a: [(0, a.shape[1])]
    for i in range(depth):
        wi = w_in[i]
        later = [wi, ffn2_w1[i], ffn2_w2[i], w_branch_attn[i], w_branch_mlstm[i], w_out[i], w_ple_gate[i],
                 w_ple_proj[i]]
        cols = [[(0, OFF_MO), (OFF_MO, OFF_MG), (OFF_BG, IN_WIDTH)]] + [whole(a) for a in later[1:]]
        h, (w_a, w_mo, w_bg, w21, w22, w_ba, w_bm, w_o, w_pg, w_pp) = _ffn(
            h, ffn1_norm_pre[i][None], bf(ffn1_w1[i]), bf(ffn1_w2[i]), ffn1_norm_post[i][None],
            casts=list(zip(later, cols)), name="ffn1")
        w_g = bf(_gate_columns(wi[:, OFF_MG:OFF_BG]))
        b_gate = _gate_columns(b_gates[i])[None]
        q, k, v, mq, mkt, mv, gcm, gb, grow = _inproj(
            h.reshape(bsz, seq, d), mix_norm_pre[i][None], w_a, w_g, b_gate, rope_tabs, conv_w[i], conv_b[i][None])
        oa = _attention(q, k, v, attn_sink[i])
        hf, hb = _mlstm(mq, mkt, mv, gcm, gb, grow)
        h = _merge(h, oa.reshape(bsz * seq, ATT_Q_W), hf.reshape(bsz * seq, M_V_W), hb.reshape(bsz * seq, M_V_W),
                   mix_norm_pre[i][None], w_mo, w_bg, w_ba, w_bm, w_o, mlstm_norm[i][None], mix_norm_post[i][None])
        h, _ = _ffn(h, ffn2_norm_pre[i][None], w21, w22, ffn2_norm_post[i][None],
                    ple=(p.reshape(depth * bsz * seq, PLE_DIM), i * bsz * seq, ple_norm_pre[i][None], w_pg, w_pp,
                         ple_norm_post[i][None]), name="ffn2_ple")
    return h.reshape(bsz, seq, d)
```

```python
import functools

import jax
import jax.numpy as jnp
from jax import lax
from jax.experimental import pallas as pl
from jax.experimental.pallas import tpu as pltpu

D_MODEL = 1024
PLE_DIM = 256
RMS_EPS = 1e-6
MACARON = 0.5
FFN_HIDDEN = 2816
ATT_HEADS = 8
ATT_KV_HEADS = 2
ATT_HEAD_DIM = 64
ATT_WINDOW = 128
ATT_BLOCK = 128
ROPE_THETA = 500000.0
ROPE_DIM = ATT_HEAD_DIM // 4
MLSTM_HEADS = 4
MLSTM_QK_DIM = 128
MLSTM_V_DIM = 256
MLSTM_CONV = 5
ATT_Q_W = ATT_HEADS * ATT_HEAD_DIM
ATT_KV_W = ATT_KV_HEADS * ATT_HEAD_DIM
M_QK_W = MLSTM_HEADS * MLSTM_QK_DIM
M_V_W = MLSTM_HEADS * MLSTM_V_DIM
M_GATE_W = 4 * MLSTM_HEADS

OFF_AQ = 0
OFF_AK = OFF_AQ + ATT_Q_W
OFF_AV = OFF_AK + ATT_KV_W
OFF_MQ = OFF_AV + ATT_KV_W
OFF_MV = OFF_MQ + 2 * M_QK_W
OFF_MO = OFF_MV + M_V_W
OFF_MG = OFF_MO + M_V_W
OFF_BG = OFF_MG + M_GATE_W
IN_WIDTH = OFF_BG + 2 * D_MODEL

LANES = 128
SUBLANES = 8
BF16_ROWS = 16
VMEM_LIMIT = 56 * 1024 * 1024

SCAN_CHUNK = 256
SCAN_STEP_CHUNKS = 4
N_STATE = 2 * MLSTM_HEADS
V_AUG = MLSTM_V_DIM + LANES
CONV_HALO = BF16_ROWS

LOG2E = 1.4426950408889634
ATT_UNIT_SUBS = 16

BF16 = jnp.bfloat16
F32 = jnp.float32


def _rms(x, g):
    return x * lax.rsqrt(jnp.mean(x * x, axis=-1, keepdims=True) + RMS_EPS) * g


def _sigmoid(x):
    return 1.0 / (1.0 + jnp.exp(-x))


def _dot(a, b):
    return jnp.dot(a, b, preferred_element_type=F32)


def _dot_nt(a, b):
    return lax.dot_general(a, b, (((1,), (1,)), ((), ())), preferred_element_type=F32)


def _ones_column_block(rows, width, col=0):
    return jnp.where(lax.broadcasted_iota(jnp.int32, (rows, width), 1) == col, 1.0, 0.0).astype(BF16)


def _ffn_kernel(sub, hid_chunk, with_ple, cast_cols, *refs):
    n_main = 10 if with_ple else 5
    n_cast = len(cast_cols)
    if with_ple:
        (h_ref, gpre_ref, w1_ref, w2_ref, gpost_ref, p_ref, gple_pre_ref, wpg_ref, wpp_ref,
         gple_post_ref) = refs[:n_main]
    else:
        h_ref, gpre_ref, w1_ref, w2_ref, gpost_ref = refs[:n_main]
    cast_in = refs[n_main:n_main + n_cast]
    o_ref = refs[n_main + n_cast]
    cast_out = iter(refs[n_main + n_cast + 1:])
    for src_ref, cols in zip(cast_in, cast_cols):
        for c0, c1 in cols:
            next(cast_out)[...] = src_ref[:, c0:c1].astype(BF16)

    n_sub = h_ref.shape[0] // sub
    rows = [slice(s * sub, (s + 1) * sub) for s in range(n_sub)]
    xn = [_rms(h_ref[r, :], gpre_ref[...]).astype(BF16) for r in rows]

    def main(s):
        acc = None
        for c in range(0, FFN_HIDDEN, hid_chunk):
            z1 = _dot(xn[s], w1_ref[:, c:c + hid_chunk])
            z2 = _dot(xn[s], w1_ref[:, FFN_HIDDEN + c:FFN_HIDDEN + c + hid_chunk])
            part = _dot((z1 * _sigmoid(z1) * z2).astype(BF16), w2_ref[c:c + hid_chunk, :])
            acc = part if acc is None else acc + part
        return acc

    def epilogue(s, acc):
        h = h_ref[rows[s], :] + MACARON * _rms(acc, gpost_ref[...])
        if with_ple:
            gate = _sigmoid(_dot(_rms(h, gple_pre_ref[...]).astype(BF16), wpg_ref[...]))
            proj = _dot(p_ref[rows[s], :].astype(BF16), wpp_ref[...])
            h = h + _rms(proj * gate, gple_post_ref[...])
        o_ref[rows[s], :] = h

    prev = main(0)
    for s in range(1, n_sub):
        cur = main(s)
        epilogue(s - 1, prev)
        prev = cur
    epilogue(n_sub - 1, prev)


def _resident(shape):
    return pl.BlockSpec(shape, lambda *_: (0,) * len(shape), pipeline_mode=pl.Buffered(1))


def _ffn(h, g_pre, w1, w2, g_post, ple=None, casts=(), *, tm=1024, sub=256, hid_chunk=2816, name="ffn"):
    t, d = h.shape
    n_step = t // tm
    row = lambda i: (i, 0)
    in_specs = [pl.BlockSpec((tm, d), row), _resident((1, d)), _resident(w1.shape), _resident(w2.shape),
                _resident((1, d))]
    args = [h, g_pre, w1, w2, g_post]
    if ple is not None:
        p, p_row0, g_ple_pre, w_gate, w_proj, g_ple_post = ple
        in_specs += [pl.BlockSpec((tm, PLE_DIM), lambda i: (i + p_row0 // tm, 0)), _resident((1, d)),
                     _resident(w_gate.shape),
                     _resident(w_proj.shape), _resident((1, d))]
        args += [p, g_ple_pre, w_gate, w_proj, g_ple_post]
    out_specs = [pl.BlockSpec((tm, d), row)]
    out_shape = [jax.ShapeDtypeStruct((t, d), F32)]
    for src, cols in casts:
        n_rows = src.shape[0]
        n_blk = max(nb for nb in range(1, n_step + 1)
                    if n_step % nb == 0 and n_rows % (nb * BF16_ROWS) == 0)
        blk_map = functools.partial(lambda spb, i: (i // spb, 0), n_step // n_blk)
        in_specs.append(pl.BlockSpec((n_rows // n_blk, src.shape[1]), blk_map))
        args.append(src)
        for c0, c1 in cols:
            out_specs.append(pl.BlockSpec((n_rows // n_blk, c1 - c0), blk_map))
            out_shape.append(jax.ShapeDtypeStruct((n_rows, c1 - c0), BF16))
    outs = pl.pallas_call(
        functools.partial(_ffn_kernel, sub, hid_chunk, ple is not None, tuple(tuple(c) for _, c in casts)),
        grid=(n_step,),
        in_specs=in_specs,
        out_specs=out_specs,
        out_shape=out_shape,
        compiler_params=pltpu.CompilerParams(dimension_semantics=("arbitrary",), vmem_limit_bytes=VMEM_LIMIT),
        name=name,
    )(*args)
    return outs[0], list(outs[1:])


WA_Q = 0
WA_KV = WA_Q + ATT_Q_W
WA_MQK = WA_KV + 2 * ATT_KV_W
WA_MV = WA_MQK + 2 * M_QK_W
WA_END = WA_MV + M_V_W


def _lane_scan(x, op, ident, reverse):
    width = x.shape[1]
    lane = lax.broadcasted_iota(jnp.int32, x.shape, 1)
    s = 1
    while s < width:
        if reverse:
            shifted = jnp.where(lane < width - s, pltpu.roll(x, width - s, 1), ident)
        else:
            shifted = jnp.where(lane >= s, pltpu.roll(x, s, 1), ident)
        x = op(x, shifted)
        s *= 2
    return x


def _inproj_kernel(n_tile, sub, h_ref, hp_ref, hn_ref, g_ref, w_ref, wg_ref, bg_ref, rc_ref, rs1_ref, rs2_ref, cw_ref,
                   cb_ref, q_ref, k_ref, v_ref, mq_ref, mkt_ref, mv_ref, gcm_ref, gb_ref, grow_ref, ue_ref, ze_ref):
    i = pl.program_id(0)
    tm = h_ref.shape[0]
    g_pre = g_ref[...]
    halo = CONV_HALO
    ue_ref[0:halo, :] = jnp.where(i > 0, _rms(hp_ref[...], g_pre), 0.0).astype(BF16)
    ue_ref[halo:halo + tm, :] = _rms(h_ref[...], g_pre).astype(BF16)
    ue_ref[halo + tm:, :] = jnp.where(i < n_tile - 1, _rms(hn_ref[...], g_pre), 0.0).astype(BF16)
    is_fwd = lax.broadcasted_iota(jnp.int32, (N_STATE, SCAN_CHUNK), 0) < MLSTM_HEADS
    cat = lambda parts: jnp.concatenate(parts, axis=1)

    n_sub = tm // sub
    rows = [slice(s * sub, (s + 1) * sub) for s in range(n_sub)]
    u_of = lambda s: ue_ref[halo + s * sub:halo + (s + 1) * sub, :]

    def attn_proj(s):
        rc, rs1, rs2 = rc_ref[rows[s], :], rs1_ref[rows[s], :], rs2_ref[rows[s], :]

        def rope(x):
            half = ROPE_DIM // 2
            return x * rc + pltpu.roll(x, LANES - half, 1) * rs1 + pltpu.roll(x, half, 1) * rs2

        zq = _dot(u_of(s), w_ref[:, WA_Q:WA_KV])
        for c in range(ATT_Q_W // LANES):
            q_ref[rows[s], c * LANES:(c + 1) * LANES] = (
                rope(zq[:, c * LANES:(c + 1) * LANES]) * (ATT_HEAD_DIM ** -0.5 * LOG2E)).astype(BF16)
        zkv = _dot(u_of(s), w_ref[:, WA_KV:WA_MQK])
        k_ref[rows[s], :] = rope(zkv[:, :ATT_KV_W]).astype(BF16)
        v_ref[rows[s], :] = zkv[:, ATT_KV_W:].astype(BF16)

    def value_proj(s):
        mv_ref[rows[s], :] = _dot(u_of(s), w_ref[:, WA_MV:WA_END]).astype(BF16)

    def qk_proj(s):
        ze_ref[s] = _dot(ue_ref[s * sub:(s + 1) * sub + 2 * halo, :], w_ref[:, WA_MQK:WA_MV])

    def qk_conv(s):
        ze = ze_ref.at[s]
        cw = cw_ref[...]
        y = cb_ref[...]
        for j in range(MLSTM_CONV):
            off = halo - MLSTM_CONV // 2 + j
            y = y + ze[off:off + sub, :] * cw[j:j + 1, :]
        qk = y * _sigmoid(y)
        mq_ref[rows[s], :] = qk[:, :M_QK_W].astype(BF16)
        mkt_ref[:, rows[s]] = (qk[:, M_QK_W:] * (MLSTM_QK_DIM ** -0.5)).T.astype(BF16)

    for s in range(n_sub):
        attn_proj(s)
        value_proj(s)
        qk_proj(s)
        qk_conv(s)
        u = u_of(s)
        rows_s = rows[s]
        gt = (_dot(u, wg_ref[...]) + bg_ref[...]).T
        li = gt[0:N_STATE]
        f_raw = gt[N_STATE:2 * N_STATE]
        lf = jnp.minimum(f_raw, 0.0) - jnp.log1p(jnp.exp(-jnp.abs(f_raw)))
        a_parts, cm_parts, b_parts = [], [], []
        for c in range(sub // SCAN_CHUNK):
            sl = slice(c * SCAN_CHUNK, (c + 1) * SCAN_CHUNK)
            lfc = lf[:, sl]
            b = jnp.where(is_fwd, _lane_scan(lfc, jnp.add, 0.0, False), _lane_scan(lfc, jnp.add, 0.0, True))
            a = li[:, sl] - b
            cm = jnp.where(is_fwd, _lane_scan(a, jnp.maximum, -jnp.inf, False),
                           _lane_scan(a, jnp.maximum, -jnp.inf, True))
            a_parts.append(a)
            cm_parts.append(cm)
            b_parts.append(b)
        grow_ref[:, rows_s] = cat(a_parts)
        pad = jnp.zeros((LANES - N_STATE, sub), F32)
        gcm_ref[rows_s, :] = jnp.concatenate([cat(cm_parts), pad], axis=0).T
        gb_ref[rows_s, :] = jnp.concatenate([cat(b_parts), pad], axis=0).T


def _inproj(h3, g_pre, w_a, w_g, b_gate, rope_tabs, conv_w, conv_b, *, tm=1024, sub=512):
    b, s, d = h3.shape
    n_tile = s // tm
    hpt = tm // CONV_HALO
    n_halo = s // CONV_HALO
    tok = lambda i, bb: (bb, i, 0)
    tok_p = lambda i, bb: (bb, jnp.maximum(i * hpt - 1, 0), 0)
    tok_n = lambda i, bb: (bb, jnp.minimum((i + 1) * hpt, n_halo - 1), 0)
    tab = lambda i, bb: (i, 0)
    qk_w = 2 * M_QK_W
    tok_spec = lambda w: pl.BlockSpec((None, tm, w), tok)
    tr_spec = lambda r: pl.BlockSpec((None, r, tm), lambda i, bb: (bb, 0, i))
    return pl.pallas_call(
        functools.partial(_inproj_kernel, n_tile, sub),
        grid=(n_tile, b),
        in_specs=[
            tok_spec(d),
            pl.BlockSpec((None, CONV_HALO, d), tok_p),
            pl.BlockSpec((None, CONV_HALO, d), tok_n),
            _resident((1, d)),
            _resident((d, WA_END)),
            _resident((d, LANES)),
            _resident((1, LANES)),
            pl.BlockSpec((tm, LANES), tab),
            pl.BlockSpec((tm, LANES), tab),
            pl.BlockSpec((tm, LANES), tab),
            _resident((MLSTM_CONV, qk_w)),
            _resident((1, qk_w)),
        ],
        out_specs=[tok_spec(ATT_Q_W), tok_spec(ATT_KV_W), tok_spec(ATT_KV_W), tok_spec(M_QK_W), tr_spec(M_QK_W),
                   tok_spec(M_V_W), tok_spec(LANES), tok_spec(LANES), tr_spec(N_STATE)],
        out_shape=[
            jax.ShapeDtypeStruct((b, s, ATT_Q_W), BF16),
            jax.ShapeDtypeStruct((b, s, ATT_KV_W), BF16),
            jax.ShapeDtypeStruct((b, s, ATT_KV_W), BF16),
            jax.ShapeDtypeStruct((b, s, M_QK_W), BF16),
            jax.ShapeDtypeStruct((b, M_QK_W, s), BF16),
            jax.ShapeDtypeStruct((b, s, M_V_W), BF16),
            jax.ShapeDtypeStruct((b, s, LANES), F32),
            jax.ShapeDtypeStruct((b, s, LANES), F32),
            jax.ShapeDtypeStruct((b, N_STATE, s), F32),
        ],
        scratch_shapes=[pltpu.VMEM((tm + 2 * CONV_HALO, d), BF16),
                        pltpu.VMEM((tm // sub, sub + 2 * CONV_HALO, qk_w), F32)],
        compiler_params=pltpu.CompilerParams(
            dimension_semantics=("arbitrary", "arbitrary"), vmem_limit_bytes=VMEM_LIMIT),
        name="inproj",
    )(h3, h3, h3, g_pre, w_a, w_g, b_gate, *rope_tabs, conv_w, conv_b)


def _attn_kernel(sink_ref, q_ref, kp_ref, kc_ref, kn_ref, vp_ref, vc_ref, vn_ref, o_ref):
    i = pl.program_id(1)
    blk = ATT_BLOCK
    tq = q_ref.shape[0]
    dh = ATT_HEAD_DIM
    group = ATT_HEADS // ATT_KV_HEADS
    rows = group * blk
    n_sub = tq // blk
    qi = lax.broadcasted_iota(jnp.int32, (rows, blk), 0) % blk
    kc = lax.broadcasted_iota(jnp.int32, (rows, blk), 1)
    bias_prev = jnp.where(kc >= qi, 0.0, -jnp.inf)
    bias_next = jnp.where(kc <= qi, 0.0, -jnp.inf)
    first_prev = jnp.where(i == 0, -jnp.inf, bias_prev)
    last_next = jnp.where(i == pl.num_programs(1) - 1, -jnp.inf, bias_next)
    head_of_row = lax.broadcasted_iota(jnp.int32, (rows, 1), 0) // blk
    kcat = jnp.concatenate([kp_ref[...], kc_ref[...], kn_ref[...]], axis=0)
    vcat = jnp.concatenate([vp_ref[...], vc_ref[...], vn_ref[...]], axis=0)
    ones_blk = _ones_column_block(3 * blk, dh)
    sinks = []
    for g in range(ATT_KV_HEADS):
        sink = jnp.zeros((rows, 1), F32)
        for hh in range(group):
            sink = jnp.where(head_of_row == hh, sink_ref[g * group + hh] * LOG2E, sink)
        sinks.append(sink)
    def scores(sb, g):
        qs = jnp.concatenate(
            [q_ref[sb * blk:(sb + 1) * blk, (g * group + hh) * dh:(g * group + hh + 1) * dh]
             for hh in range(group)], axis=0)
        s = _dot_nt(qs, kcat[sb * blk:(sb + 3) * blk, g * dh:(g + 1) * dh])
        bp = first_prev if sb == 0 else bias_prev
        bn = last_next if sb == n_sub - 1 else bias_next
        return jnp.concatenate([s[:, :blk] + bp, s[:, blk:2 * blk], s[:, 2 * blk:] + bn], axis=1)

    units = [(sb, g) for sb in range(n_sub) for g in range(ATT_KV_HEADS)]
    ss = [scores(sb, g) for sb, g in units]
    mxs = [jnp.maximum(jnp.max(s, axis=-1, keepdims=True), sinks[g]) for s, (sb, g) in zip(ss, units)]
    pvs = [_dot(jnp.exp2(s - mx).astype(BF16),
                jnp.concatenate([vcat[sb * blk:(sb + 3) * blk, g * dh:(g + 1) * dh], ones_blk], axis=1))
           for s, mx, (sb, g) in zip(ss, mxs, units)]
    os_ = [pv[:, :dh] / (pv[:, dh:dh + 1] + jnp.exp2(sinks[g] - mx)) for pv, mx, (sb, g) in zip(pvs, mxs, units)]
    for sb in range(n_sub):
        outs = [os_[sb * ATT_KV_HEADS + g][hh * blk:(hh + 1) * blk]
                for g in range(ATT_KV_HEADS) for hh in range(group)]
        o_ref[sb * blk:(sb + 1) * blk, :] = jnp.concatenate(outs, axis=-1).astype(BF16)


def _attention(q, k, v, sink, *, tq=ATT_UNIT_SUBS * ATT_BLOCK):
    b, s, _ = q.shape
    n_tile = s // tq
    bpt = tq // ATT_BLOCK
    n_blk = s // ATT_BLOCK
    cur = lambda bb, i: (bb, i, 0)
    prev = lambda bb, i: (bb, jnp.maximum(i * bpt - 1, 0), 0)
    nxt = lambda bb, i: (bb, jnp.minimum((i + 1) * bpt, n_blk - 1), 0)
    halo_spec = lambda im: pl.BlockSpec((None, ATT_BLOCK, ATT_KV_W), im)
    main_spec = pl.BlockSpec((None, tq, ATT_KV_W), cur)
    return pl.pallas_call(
        _attn_kernel,
        grid=(b, n_tile),
        in_specs=[
            pl.BlockSpec(memory_space=pltpu.SMEM),
            pl.BlockSpec((None, tq, ATT_Q_W), cur),
            halo_spec(prev), main_spec, halo_spec(nxt),
            halo_spec(prev), main_spec, halo_spec(nxt),
        ],
        out_specs=pl.BlockSpec((None, tq, ATT_Q_W), cur),
        out_shape=jax.ShapeDtypeStruct((b, s, ATT_Q_W), BF16),
        compiler_params=pltpu.CompilerParams(
            dimension_semantics=("arbitrary", "arbitrary"), vmem_limit_bytes=VMEM_LIMIT),
        name="attn",
    )(sink, q, k, k, k, v, v, v)


def _mlstm_kernel(qf_ref, ktf_ref, vf_ref, gmf_ref, gbf_ref, grf_ref,
                  qb_ref, ktb_ref, vb_ref, gmb_ref, gbb_ref, grb_ref,
                  of_ref, ob_ref, c_ref, m_ref):
    c = pl.program_id(1)
    L = SCAN_CHUNK
    dk = MLSTM_QK_DIM
    dv = MLSTM_V_DIM

    @pl.when(c == 0)
    def _():
        c_ref[...] = jnp.zeros_like(c_ref)
        m_ref[...] = jnp.zeros_like(m_ref)

    row = lax.broadcasted_iota(jnp.int32, (L, L), 0)
    col = lax.broadcasted_iota(jnp.int32, (L, L), 1)
    dirs = (
        (0, qf_ref, ktf_ref, vf_ref, gmf_ref, gbf_ref, grf_ref, of_ref),
        (1, qb_ref, ktb_ref, vb_ref, gmb_ref, gbb_ref, grb_ref, ob_ref),
    )
    n_sc = qf_ref.shape[0] // L
    order = [(d, j if d == 0 else n_sc - 1 - j) for j in range(n_sc) for d in range(2)]
    for d, j in order:
        _, q_ref, kt_ref, v_ref, gm_ref, gb_ref, gr_ref, o_ref = dirs[d]
        tsl = slice(j * L, (j + 1) * L)
        causal = (col <= row) if d == 0 else (col >= row)
        last = L - 1 if d == 0 else 0
        m_prev_v = m_ref[d:d + 1, :]
        m_t = jnp.maximum(gm_ref[tsl, :], m_prev_v)
        w_inter_t = jnp.exp(m_prev_v - m_t)
        gb_t = gb_ref[tsl, :]
        den_min_t = jnp.exp(-(gb_t + m_t))
        m_last_v = m_t[last:last + 1, :]
        decay_v = jnp.exp(m_prev_v - m_last_v)
        m_ref[d:d + 1, :] = gb_t[last:last + 1, :] + m_last_v
        heads = range(MLSTM_HEADS)
        rs = [d * MLSTM_HEADS + h for h in heads]
        qs = [q_ref[tsl, h * dk:(h + 1) * dk] for h in heads]
        kts = [kt_ref[h * dk:(h + 1) * dk, tsl] for h in heads]
        v_augs = [jnp.concatenate([v_ref[tsl, h * dv:(h + 1) * dv], _ones_column_block(L, LANES, rs[h])], axis=1)
                  for h in heads]
        a_rs = [gr_ref[r:r + 1, tsl] for r in rs]
        c_prevs = [c_ref[r] for r in rs]
        w_intras = [jnp.exp(jnp.where(causal, a_rs[h] - m_t[:, rs[h]:rs[h] + 1], -jnp.inf)) for h in heads]
        s_raws = [_dot(qs[h], kts[h]) for h in heads]
        scs = [(s_raws[h] * w_intras[h]).astype(BF16) for h in heads]
        q_inters = [(qs[h].astype(F32) * w_inter_t[:, rs[h]:rs[h] + 1]).astype(BF16) for h in heads]
        tots = [_dot(jnp.concatenate([scs[h], q_inters[h]], axis=1),
                     jnp.concatenate([v_augs[h], c_prevs[h].astype(BF16)], axis=0)) for h in heads]
        for h in heads:
            r = rs[h]
            den = jnp.maximum(jnp.abs(tots[h][:, dv:]), den_min_t)
            o_ref[tsl, h * dv:(h + 1) * dv] = (tots[h][:, :dv] * (1.0 / den)[:, r:r + 1]).astype(BF16)
        for h in heads:
            r = rs[h]
            w_k = jnp.exp(a_rs[h] - m_last_v[:, r:r + 1])
            kw = (kts[h].astype(F32) * w_k).astype(BF16)
            c_ref[r] = decay_v[:, r:r + 1] * c_prevs[h] + _dot(kw, v_augs[h])


def _mlstm(mq, mkt, mv, gcm, gb, grow):
    b, s, _ = mq.shape
    L = SCAN_CHUNK * SCAN_STEP_CHUNKS
    n_chunk = s // L
    fwd = lambda bb, c: (bb, c, 0)
    bwd = lambda bb, c: (bb, n_chunk - 1 - c, 0)
    fwd_t = lambda bb, c: (bb, 0, c)
    bwd_t = lambda bb, c: (bb, 0, n_chunk - 1 - c)

    def dir_specs(tok, tr):
        return [
            pl.BlockSpec((None, L, M_QK_W), tok),
            pl.BlockSpec((None, M_QK_W, L), tr),
            pl.BlockSpec((None, L, M_V_W), tok),
            pl.BlockSpec((None, L, LANES), tok),
            pl.BlockSpec((None, L, LANES), tok),
            pl.BlockSpec((None, N_STATE, L), tr),
        ]

    return pl.pallas_call(
        _mlstm_kernel,
        grid=(b, n_chunk),
        in_specs=dir_specs(fwd, fwd_t) + dir_specs(bwd, bwd_t),
        out_specs=[pl.BlockSpec((None, L, M_V_W), fwd), pl.BlockSpec((None, L, M_V_W), bwd)],
        out_shape=[jax.ShapeDtypeStruct((b, s, M_V_W), BF16)] * 2,
        scratch_shapes=[
            pltpu.VMEM((N_STATE, MLSTM_QK_DIM, V_AUG), F32),
            pltpu.VMEM((SUBLANES, LANES), F32),
        ],
        compiler_params=pltpu.CompilerParams(
            dimension_semantics=("arbitrary", "arbitrary"), vmem_limit_bytes=VMEM_LIMIT),
        name="mlstm",
    )(mq, mkt, mv, gcm, gb, grow, mq, mkt, mv, gcm, gb, grow)


def _merge_kernel(sub, h_ref, oa_ref, hf_ref, hb_ref, gpre_ref, wmo_ref, wbg_ref, wba_ref, wbm_ref, wout_ref,
                  gm_ref, gpost_ref, o_ref):
    dv = MLSTM_V_DIM
    n_sub = h_ref.shape[0] // sub
    rows = [slice(s * sub, (s + 1) * sub) for s in range(n_sub)]

    def prologue(s):
        u = _rms(h_ref[rows[s], :], gpre_ref[...]).astype(BF16)
        hm = hf_ref[rows[s], :].astype(F32) + hb_ref[rows[s], :].astype(F32)
        hm = jnp.concatenate(
            [hm[:, i * dv:(i + 1) * dv]
             * lax.rsqrt(jnp.mean(hm[:, i * dv:(i + 1) * dv] ** 2, axis=-1, keepdims=True) + RMS_EPS)
             for i in range(MLSTM_HEADS)], axis=-1)
        return u, hm * gm_ref[...]

    subs = range(n_sub)
    pre = [prologue(s) for s in subs]
    us = [p[0] for p in pre]
    hms = [(pre[s][1] * _sigmoid(_dot(us[s], wmo_ref[...]))).astype(BF16) for s in subs]
    yms = [_dot(hms[s], wbm_ref[...]) for s in subs]
    yas = [_dot(oa_ref[rows[s], :], wba_ref[...]) for s in subs]
    merged = [(_sigmoid(_dot(us[s], wbg_ref[:, :D_MODEL])) * yas[s]
               + _sigmoid(_dot(us[s], wbg_ref[:, D_MODEL:])) * yms[s]).astype(BF16) for s in subs]
    mixes = [_dot(merged[s], wout_ref[...]) for s in subs]
    for s in subs:
        o_ref[rows[s], :] = h_ref[rows[s], :] + _rms(mixes[s], gpost_ref[...])


def _merge(h, oa, hf, hb, g_pre, w_mo, w_bg, w_ba, w_bm, w_out, g_m, g_post, *, tm=1024, sub=256):
    t, d = h.shape
    row = lambda i: (i, 0)
    full = lambda a: _resident(a.shape)
    return pl.pallas_call(
        functools.partial(_merge_kernel, sub),
        grid=(t // tm,),
        in_specs=[
            pl.BlockSpec((tm, d), row), pl.BlockSpec((tm, ATT_Q_W), row),
            pl.BlockSpec((tm, M_V_W), row), pl.BlockSpec((tm, M_V_W), row),
            full(g_pre), full(w_mo), full(w_bg), full(w_ba), full(w_bm), full(w_out), full(g_m), full(g_post),
        ],
        out_specs=pl.BlockSpec((tm, d), row),
        out_shape=jax.ShapeDtypeStruct((t, d), F32),
        compiler_params=pltpu.CompilerParams(
            dimension_semantics=("arbitrary",), vmem_limit_bytes=VMEM_LIMIT),
        name="merge",
    )(h, oa, hf, hb, g_pre, w_mo, w_bg, w_ba, w_bm, w_out, g_m, g_post)


def _rope_tables(seq):
    pos = jnp.arange(seq, dtype=F32)
    inv = ROPE_THETA ** (-jnp.arange(0, ROPE_DIM, 2, dtype=F32) / ROPE_DIM)
    ang = pos[:, None] * inv[None, :]
    cos, sin = jnp.cos(ang), jnp.sin(ang)
    half = ROPE_DIM // 2
    dpos = jnp.arange(LANES) % ATT_HEAD_DIM
    sel = dpos % half
    rc = jnp.where(dpos < ROPE_DIM, cos[:, sel], 1.0)
    rs1 = jnp.where(dpos < half, -sin[:, sel], 0.0)
    rs2 = jnp.where((dpos >= half) & (dpos < ROPE_DIM), sin[:, sel], 0.0)
    return rc, rs1, rs2


def _gate_columns(a):
    g = a.reshape(a.shape[:-1] + (4, MLSTM_HEADS))
    g = jnp.stack([g[..., 0, :], g[..., 2, :], g[..., 1, :], g[..., 3, :]], axis=-2)
    g = g.reshape(a.shape[:-1] + (M_GATE_W,))
    return jnp.pad(g, [(0, 0)] * (a.ndim - 1) + [(0, LANES - M_GATE_W)])


def kernel(x, p, ffn1_norm_pre, ffn1_w1, ffn1_w2, ffn1_norm_post, mix_norm_pre, w_in, b_gates, conv_w, conv_b,
           attn_sink, mlstm_norm, w_branch_attn, w_branch_mlstm, w_out, mix_norm_post, ffn2_norm_pre, ffn2_w1,
           ffn2_w2, ffn2_norm_post, ple_norm_pre, w_ple_gate, w_ple_proj, ple_norm_post):
    bsz, seq, d = x.shape
    depth = p.shape[0]
    h = x.reshape(bsz * seq, d)
    rope_tabs = _rope_tables(seq)
    bf = lambda a: a.astype(BF16)
    whole = lambda a: [(0, a.shape[1])]
    for i in range(depth):
        wi = w_in[i]
        later = [wi, ffn2_w1[i], ffn2_w2[i], w_branch_attn[i], w_branch_mlstm[i], w_out[i], w_ple_gate[i],
                 w_ple_proj[i]]
        cols = [[(0, OFF_MO), (OFF_MO, OFF_MG), (OFF_BG, IN_WIDTH)]] + [whole(a) for a in later[1:]]
        h, (w_a, w_mo, w_bg, w21, w22, w_ba, w_bm, w_o, w_pg, w_pp) = _ffn(
            h, ffn1_norm_pre[i][None], bf(ffn1_w1[i]), bf(ffn1_w2[i]), ffn1_norm_post[i][None],
            casts=list(zip(later, cols)), name="ffn1")
        w_g = bf(_gate_columns(wi[:, OFF_MG:OFF_BG]))
        b_gate = _gate_columns(b_gates[i])[None]
        q, k, v, mq, mkt, mv, gcm, gb, grow = _inproj(
            h.reshape(bsz, seq, d), mix_norm_pre[i][None], w_a, w_g, b_gate, rope_tabs, conv_w[i], conv_b[i][None])
        oa = _attention(q, k, v, attn_sink[i])
        hf, hb = _mlstm(mq, mkt, mv, gcm, gb, grow)
        h = _merge(h, oa.reshape(bsz * seq, ATT_Q_W), hf.reshape(bsz * seq, M_V_W), hb.reshape(bsz * seq, M_V_W),
                   mix_norm_pre[i][None], w_mo, w_bg, w_ba, w_bm, w_o, mlstm_norm[i][None], mix_norm_post[i][None])
        h, _ = _ffn(h, ffn2_norm_pre[i][None], w21, w22, ffn2_norm_post[i][None],
                    ple=(p.reshape(depth * bsz * seq, PLE_DIM), i * bsz * seq, ple_norm_pre[i][None], w_pg, w_pp,
                         ple_norm_post[i][None]), name="ffn2_ple")
    return h.reshape(bsz, seq, d)
```

```python
import functools

import jax
import jax.numpy as jnp
from jax import lax
from jax.experimental import pallas as pl
from jax.experimental.pallas import tpu as pltpu

D_MODEL = 1024
PLE_DIM = 256
RMS_EPS = 1e-6
MACARON = 0.5
FFN_HIDDEN = 2816
ATT_HEADS = 8
ATT_KV_HEADS = 2
ATT_HEAD_DIM = 64
ATT_WINDOW = 128
ATT_BLOCK = 128
ROPE_THETA = 500000.0
ROPE_DIM = ATT_HEAD_DIM // 4
MLSTM_HEADS = 4
MLSTM_QK_DIM = 128
MLSTM_V_DIM = 256
MLSTM_CONV = 5
ATT_Q_W = ATT_HEADS * ATT_HEAD_DIM
ATT_KV_W = ATT_KV_HEADS * ATT_HEAD_DIM
M_QK_W = MLSTM_HEADS * MLSTM_QK_DIM
M_V_W = MLSTM_HEADS * MLSTM_V_DIM
M_GATE_W = 4 * MLSTM_HEADS

OFF_AQ = 0
OFF_AK = OFF_AQ + ATT_Q_W
OFF_AV = OFF_AK + ATT_KV_W
OFF_MQ = OFF_AV + ATT_KV_W
OFF_MV = OFF_MQ + 2 * M_QK_W
OFF_MO = OFF_MV + M_V_W
OFF_MG = OFF_MO + M_V_W
OFF_BG = OFF_MG + M_GATE_W
IN_WIDTH = OFF_BG + 2 * D_MODEL

LANES = 128
SUBLANES = 8
BF16_ROWS = 16
VMEM_LIMIT = 56 * 1024 * 1024

SCAN_CHUNK = 256
SCAN_STEP_CHUNKS = 4
N_STATE = 2 * MLSTM_HEADS
V_AUG = MLSTM_V_DIM + LANES
CONV_HALO = BF16_ROWS

LOG2E = 1.4426950408889634
ATT_UNIT_SUBS = 16

BF16 = jnp.bfloat16
F32 = jnp.float32


def _rms(x, g):
    return x * lax.rsqrt(jnp.mean(x * x, axis=-1, keepdims=True) + RMS_EPS) * g


def _sigmoid(x):
    return 1.0 / (1.0 + jnp.exp(-x))


def _dot(a, b):
    return jnp.dot(a, b, preferred_element_type=F32)


def _dot_nt(a, b):
    return lax.dot_general(a, b, (((1,), (1,)), ((), ())), preferred_element_type=F32)


def _ones_column_block(rows, width, col=0):
    return jnp.where(lax.broadcasted_iota(jnp.int32, (rows, width), 1) == col, 1.0, 0.0).astype(BF16)


def _ffn_kernel(sub, hid_chunk, with_ple, cast_cols, *refs):
    n_main = 10 if with_ple else 5
    n_cast = len(cast_cols)
    if with_ple:
        (h_ref, gpre_ref, w1_ref, w2_ref, gpost_ref, p_ref, gple_pre_ref, wpg_ref, wpp_ref,
         gple_post_ref) = refs[:n_main]
    else:
        h_ref, gpre_ref, w1_ref, w2_ref, gpost_ref = refs[:n_main]
    cast_in = refs[n_main:n_main + n_cast]
    o_ref = refs[n_main + n_cast]
    cast_out = iter(refs[n_main + n_cast + 1:])
    for src_ref, cols in zip(cast_in, cast_cols):
        for c0, c1 in cols:
            next(cast_out)[...] = src_ref[:, c0:c1].astype(BF16)

    n_sub = h_ref.shape[0] // sub
    rows = [slice(s * sub, (s + 1) * sub) for s in range(n_sub)]
    xn = [_rms(h_ref[r, :], gpre_ref[...]).astype(BF16) for r in rows]

    def main(s):
        acc = None
        for c in range(0, FFN_HIDDEN, hid_chunk):
            z1 = _dot(xn[s], w1_ref[:, c:c + hid_chunk])
            z2 = _dot(xn[s], w1_ref[:, FFN_HIDDEN + c:FFN_HIDDEN + c + hid_chunk])
            part = _dot((z1 * _sigmoid(z1) * z2).astype(BF16), w2_ref[c:c + hid_chunk, :])
            acc = part if acc is None else acc + part
        return acc

    def epilogue(s, acc):
        h = h_ref[rows[s], :] + MACARON * _rms(acc, gpost_ref[...])
        if with_ple:
            gate = _sigmoid(_dot(_rms(h, gple_pre_ref[...]).astype(BF16), wpg_ref[...]))
            proj = _dot(p_ref[rows[s], :].astype(BF16), wpp_ref[...])
            h = h + _rms(proj * gate, gple_post_ref[...])
        o_ref[rows[s], :] = h

    prev = main(0)
    for s in range(1, n_sub):
        cur = main(s)
        epilogue(s - 1, prev)
        prev = cur
    epilogue(n_sub - 1, prev)


def _resident(shape):
    return pl.BlockSpec(shape, lambda *_: (0,) * len(shape), pipeline_mode=pl.Buffered(1))


def _ffn(h, g_pre, w1, w2, g_post, ple=None, casts=(), *, tm=1024, sub=256, hid_chunk=2816, name="ffn"):
    t, d = h.shape
    assert t % tm == 0 and tm % sub == 0, (t, tm, sub)
    n_step = t // tm
    row = lambda i: (i, 0)
    in_specs = [pl.BlockSpec((tm, d), row), _resident((1, d)), _resident(w1.shape), _resident(w2.shape),
                _resident((1, d))]
    args = [h, g_pre, w1, w2, g_post]
    if ple is not None:
        p, p_row0, g_ple_pre, w_gate, w_proj, g_ple_post = ple
        in_specs += [pl.BlockSpec((tm, PLE_DIM), lambda i: (i + p_row0 // tm, 0)), _resident((1, d)),
                     _resident(w_gate.shape),
                     _resident(w_proj.shape), _resident((1, d))]
        args += [p, g_ple_pre, w_gate, w_proj, g_ple_post]
    out_specs = [pl.BlockSpec((tm, d), row)]
    out_shape = [jax.ShapeDtypeStruct((t, d), F32)]
    for src, cols in casts:
        n_rows = src.shape[0]
        n_blk = max(nb for nb in range(1, n_step + 1)
                    if n_step % nb == 0 and n_rows % (nb * BF16_ROWS) == 0)
        blk_map = functools.partial(lambda spb, i: (i // spb, 0), n_step // n_blk)
        in_specs.append(pl.BlockSpec((n_rows // n_blk, src.shape[1]), blk_map))
        args.append(src)
        for c0, c1 in cols:
            out_specs.append(pl.BlockSpec((n_rows // n_blk, c1 - c0), blk_map))
            out_shape.append(jax.ShapeDtypeStruct((n_rows, c1 - c0), BF16))
    outs = pl.pallas_call(
        functools.partial(_ffn_kernel, sub, hid_chunk, ple is not None, tuple(tuple(c) for _, c in casts)),
        grid=(n_step,),
        in_specs=in_specs,
        out_specs=out_specs,
        out_shape=out_shape,
        compiler_params=pltpu.CompilerParams(dimension_semantics=("arbitrary",), vmem_limit_bytes=VMEM_LIMIT),
        name=name,
    )(*args)
    return outs[0], list(outs[1:])


WA_Q = 0
WA_KV = WA_Q + ATT_Q_W
WA_MQK = WA_KV + 2 * ATT_KV_W
WA_MV = WA_MQK + 2 * M_QK_W
WA_END = WA_MV + M_V_W


def _lane_scan(x, op, ident, reverse):
    width = x.shape[1]
    lane = lax.broadcasted_iota(jnp.int32, x.shape, 1)
    s = 1
    while s < width:
        if reverse:
            shifted = jnp.where(lane < width - s, pltpu.roll(x, width - s, 1), ident)
        else:
            shifted = jnp.where(lane >= s, pltpu.roll(x, s, 1), ident)
        x = op(x, shifted)
        s *= 2
    return x


def _inproj_kernel(n_tile, sub, h_ref, hp_ref, hn_ref, g_ref, w_ref, wg_ref, bg_ref, rc_ref, rs1_ref, rs2_ref, cw_ref,
                   cb_ref, q_ref, k_ref, v_ref, mq_ref, mkt_ref, mv_ref, gcm_ref, gb_ref, grow_ref, ue_ref, ze_ref):
    i = pl.program_id(0)
    tm = h_ref.shape[0]
    g_pre = g_ref[...]
    halo = CONV_HALO
    ue_ref[0:halo, :] = jnp.where(i > 0, _rms(hp_ref[...], g_pre), 0.0).astype(BF16)
    ue_ref[halo:halo + tm, :] = _rms(h_ref[...], g_pre).astype(BF16)
    ue_ref[halo + tm:, :] = jnp.where(i < n_tile - 1, _rms(hn_ref[...], g_pre), 0.0).astype(BF16)
    is_fwd = lax.broadcasted_iota(jnp.int32, (N_STATE, SCAN_CHUNK), 0) < MLSTM_HEADS
    cat = lambda parts: jnp.concatenate(parts, axis=1)

    n_sub = tm // sub
    rows = [slice(s * sub, (s + 1) * sub) for s in range(n_sub)]
    u_of = lambda s: ue_ref[halo + s * sub:halo + (s + 1) * sub, :]

    def attn_proj(s):
        rc, rs1, rs2 = rc_ref[rows[s], :], rs1_ref[rows[s], :], rs2_ref[rows[s], :]

        def rope(x):
            half = ROPE_DIM // 2
            return x * rc + pltpu.roll(x, LANES - half, 1) * rs1 + pltpu.roll(x, half, 1) * rs2

        zq = _dot(u_of(s), w_ref[:, WA_Q:WA_KV])
        for c in range(ATT_Q_W // LANES):
            q_ref[rows[s], c * LANES:(c + 1) * LANES] = (
                rope(zq[:, c * LANES:(c + 1) * LANES]) * (ATT_HEAD_DIM ** -0.5 * LOG2E)).astype(BF16)
        zkv = _dot(u_of(s), w_ref[:, WA_KV:WA_MQK])
        k_ref[rows[s], :] = rope(zkv[:, :ATT_KV_W]).astype(BF16)
        v_ref[rows[s], :] = zkv[:, ATT_KV_W:].astype(BF16)

    def value_proj(s):
        mv_ref[rows[s], :] = _dot(u_of(s), w_ref[:, WA_MV:WA_END]).astype(BF16)

    def qk_proj(s):
        ze_ref[s] = _dot(ue_ref[s * sub:(s + 1) * sub + 2 * halo, :], w_ref[:, WA_MQK:WA_MV])

    def qk_conv(s):
        ze = ze_ref.at[s]
        cw = cw_ref[...]
        y = cb_ref[...]
        for j in range(MLSTM_CONV):
            off = halo - MLSTM_CONV // 2 + j
            y = y + ze[off:off + sub, :] * cw[j:j + 1, :]
        qk = y * _sigmoid(y)
        mq_ref[rows[s], :] = qk[:, :M_QK_W].astype(BF16)
        mkt_ref[:, rows[s]] = (qk[:, M_QK_W:] * (MLSTM_QK_DIM ** -0.5)).T.astype(BF16)

    for s in range(n_sub):
        attn_proj(s)
        value_proj(s)
        qk_proj(s)
        qk_conv(s)
        u = u_of(s)
        rows_s = rows[s]
        gt = (_dot(u, wg_ref[...]) + bg_ref[...]).T
        li = gt[0:N_STATE]
        f_raw = gt[N_STATE:2 * N_STATE]
        lf = jnp.minimum(f_raw, 0.0) - jnp.log1p(jnp.exp(-jnp.abs(f_raw)))
        a_parts, cm_parts, b_parts = [], [], []
        for c in range(sub // SCAN_CHUNK):
            sl = slice(c * SCAN_CHUNK, (c + 1) * SCAN_CHUNK)
            lfc = lf[:, sl]
            b = jnp.where(is_fwd, _lane_scan(lfc, jnp.add, 0.0, False), _lane_scan(lfc, jnp.add, 0.0, True))
            a = li[:, sl] - b
            cm = jnp.where(is_fwd, _lane_scan(a, jnp.maximum, -jnp.inf, False),
                           _lane_scan(a, jnp.maximum, -jnp.inf, True))
            a_parts.append(a)
            cm_parts.append(cm)
            b_parts.append(b)
        grow_ref[:, rows_s] = cat(a_parts)
        pad = jnp.zeros((LANES - N_STATE, sub), F32)
        gcm_ref[rows_s, :] = jnp.concatenate([cat(cm_parts), pad], axis=0).T
        gb_ref[rows_s, :] = jnp.concatenate([cat(b_parts), pad], axis=0).T


def _inproj(h3, g_pre, w_a, w_g, b_gate, rope_tabs, conv_w, conv_b, *, tm=1024, sub=512):
    b, s, d = h3.shape
    assert s % tm == 0 and tm % sub == 0 and sub % SCAN_CHUNK == 0, (s, tm, sub)
    n_tile = s // tm
    hpt = tm // CONV_HALO
    n_halo = s // CONV_HALO
    tok = lambda i, bb: (bb, i, 0)
    tok_p = lambda i, bb: (bb, jnp.maximum(i * hpt - 1, 0), 0)
    tok_n = lambda i, bb: (bb, jnp.minimum((i + 1) * hpt, n_halo - 1), 0)
    tab = lambda i, bb: (i, 0)
    qk_w = 2 * M_QK_W
    tok_spec = lambda w: pl.BlockSpec((None, tm, w), tok)
    tr_spec = lambda r: pl.BlockSpec((None, r, tm), lambda i, bb: (bb, 0, i))
    return pl.pallas_call(
        functools.partial(_inproj_kernel, n_tile, sub),
        grid=(n_tile, b),
        in_specs=[
            tok_spec(d),
            pl.BlockSpec((None, CONV_HALO, d), tok_p),
            pl.BlockSpec((None, CONV_HALO, d), tok_n),
            _resident((1, d)),
            _resident((d, WA_END)),
            _resident((d, LANES)),
            _resident((1, LANES)),
            pl.BlockSpec((tm, LANES), tab),
            pl.BlockSpec((tm, LANES), tab),
            pl.BlockSpec((tm, LANES), tab),
            _resident((MLSTM_CONV, qk_w)),
            _resident((1, qk_w)),
        ],
        out_specs=[tok_spec(ATT_Q_W), tok_spec(ATT_KV_W), tok_spec(ATT_KV_W), tok_spec(M_QK_W), tr_spec(M_QK_W),
                   tok_spec(M_V_W), tok_spec(LANES), tok_spec(LANES), tr_spec(N_STATE)],
        out_shape=[
            jax.ShapeDtypeStruct((b, s, ATT_Q_W), BF16),
            jax.ShapeDtypeStruct((b, s, ATT_KV_W), BF16),
            jax.ShapeDtypeStruct((b, s, ATT_KV_W), BF16),
            jax.ShapeDtypeStruct((b, s, M_QK_W), BF16),
            jax.ShapeDtypeStruct((b, M_QK_W, s), BF16),
            jax.ShapeDtypeStruct((b, s, M_V_W), BF16),
            jax.ShapeDtypeStruct((b, s, LANES), F32),
            jax.ShapeDtypeStruct((b, s, LANES), F32),
            jax.ShapeDtypeStruct((b, N_STATE, s), F32),
        ],
        scratch_shapes=[pltpu.VMEM((tm + 2 * CONV_HALO, d), BF16),
                        pltpu.VMEM((tm // sub, sub + 2 * CONV_HALO, qk_w), F32)],
        compiler_params=pltpu.CompilerParams(
            dimension_semantics=("arbitrary", "arbitrary"), vmem_limit_bytes=VMEM_LIMIT),
        name="inproj",
    )(h3, h3, h3, g_pre, w_a, w_g, b_gate, *rope_tabs, conv_w, conv_b)


def _attn_kernel(sink_ref, q_ref, kp_ref, kc_ref, kn_ref, vp_ref, vc_ref, vn_ref, o_ref):
    i = pl.program_id(1)
    blk = ATT_BLOCK
    tq = q_ref.shape[0]
    dh = ATT_HEAD_DIM
    group = ATT_HEADS // ATT_KV_HEADS
    rows = group * blk
    n_sub = tq // blk
    qi = lax.broadcasted_iota(jnp.int32, (rows, blk), 0) % blk
    kc = lax.broadcasted_iota(jnp.int32, (rows, blk), 1)
    bias_prev = jnp.where(kc >= qi, 0.0, -jnp.inf)
    bias_next = jnp.where(kc <= qi, 0.0, -jnp.inf)
    first_prev = jnp.where(i == 0, -jnp.inf, bias_prev)
    last_next = jnp.where(i == pl.num_programs(1) - 1, -jnp.inf, bias_next)
    head_of_row = lax.broadcasted_iota(jnp.int32, (rows, 1), 0) // blk
    kcat = jnp.concatenate([kp_ref[...], kc_ref[...], kn_ref[...]], axis=0)
    vcat = jnp.concatenate([vp_ref[...], vc_ref[...], vn_ref[...]], axis=0)
    ones_blk = _ones_column_block(3 * blk, dh)
    sinks = []
    for g in range(ATT_KV_HEADS):
        sink = jnp.zeros((rows, 1), F32)
        for hh in range(group):
            sink = jnp.where(head_of_row == hh, sink_ref[g * group + hh] * LOG2E, sink)
        sinks.append(sink)
    def scores(sb, g):
        qs = jnp.concatenate(
            [q_ref[sb * blk:(sb + 1) * blk, (g * group + hh) * dh:(g * group + hh + 1) * dh]
             for hh in range(group)], axis=0)
        s = _dot_nt(qs, kcat[sb * blk:(sb + 3) * blk, g * dh:(g + 1) * dh])
        bp = first_prev if sb == 0 else bias_prev
        bn = last_next if sb == n_sub - 1 else bias_next
        return jnp.concatenate([s[:, :blk] + bp, s[:, blk:2 * blk], s[:, 2 * blk:] + bn], axis=1)

    units = [(sb, g) for sb in range(n_sub) for g in range(ATT_KV_HEADS)]
    ss = [scores(sb, g) for sb, g in units]
    mxs = [jnp.maximum(jnp.max(s, axis=-1, keepdims=True), sinks[g]) for s, (sb, g) in zip(ss, units)]
    pvs = [_dot(jnp.exp2(s - mx).astype(BF16),
                jnp.concatenate([vcat[sb * blk:(sb + 3) * blk, g * dh:(g + 1) * dh], ones_blk], axis=1))
           for s, mx, (sb, g) in zip(ss, mxs, units)]
    os_ = [pv[:, :dh] / (pv[:, dh:dh + 1] + jnp.exp2(sinks[g] - mx)) for pv, mx, (sb, g) in zip(pvs, mxs, units)]
    for sb in range(n_sub):
        outs = [os_[sb * ATT_KV_HEADS + g][hh * blk:(hh + 1) * blk]
                for g in range(ATT_KV_HEADS) for hh in range(group)]
        o_ref[sb * blk:(sb + 1) * blk, :] = jnp.concatenate(outs, axis=-1).astype(BF16)


def _attention(q, k, v, sink, *, tq=ATT_UNIT_SUBS * ATT_BLOCK):
    b, s, _ = q.shape
    assert s % tq == 0 and tq % ATT_BLOCK == 0, (s, tq)
    n_tile = s // tq
    bpt = tq // ATT_BLOCK
    n_blk = s // ATT_BLOCK
    cur = lambda bb, i: (bb, i, 0)
    prev = lambda bb, i: (bb, jnp.maximum(i * bpt - 1, 0), 0)
    nxt = lambda bb, i: (bb, jnp.minimum((i + 1) * bpt, n_blk - 1), 0)
    halo_spec = lambda im: pl.BlockSpec((None, ATT_BLOCK, ATT_KV_W), im)
    main_spec = pl.BlockSpec((None, tq, ATT_KV_W), cur)
    return pl.pallas_call(
        _attn_kernel,
        grid=(b, n_tile),
        in_specs=[
            pl.BlockSpec(memory_space=pltpu.SMEM),
            pl.BlockSpec((None, tq, ATT_Q_W), cur),
            halo_spec(prev), main_spec, halo_spec(nxt),
            halo_spec(prev), main_spec, halo_spec(nxt),
        ],
        out_specs=pl.BlockSpec((None, tq, ATT_Q_W), cur),
        out_shape=jax.ShapeDtypeStruct((b, s, ATT_Q_W), BF16),
        compiler_params=pltpu.CompilerParams(
            dimension_semantics=("arbitrary", "arbitrary"), vmem_limit_bytes=VMEM_LIMIT),
        name="attn",
    )(sink, q, k, k, k, v, v, v)


def _mlstm_kernel(qf_ref, ktf_ref, vf_ref, gmf_ref, gbf_ref, grf_ref,
                  qb_ref, ktb_ref, vb_ref, gmb_ref, gbb_ref, grb_ref,
                  of_ref, ob_ref, c_ref, m_ref):
    c = pl.program_id(1)
    L = SCAN_CHUNK
    dk = MLSTM_QK_DIM
    dv = MLSTM_V_DIM

    @pl.when(c == 0)
    def _():
        c_ref[...] = jnp.zeros_like(c_ref)
        m_ref[...] = jnp.zeros_like(m_ref)

    row = lax.broadcasted_iota(jnp.int32, (L, L), 0)
    col = lax.broadcasted_iota(jnp.int32, (L, L), 1)
    dirs = (
        (0, qf_ref, ktf_ref, vf_ref, gmf_ref, gbf_ref, grf_ref, of_ref),
        (1, qb_ref, ktb_ref, vb_ref, gmb_ref, gbb_ref, grb_ref, ob_ref),
    )
    n_sc = qf_ref.shape[0] // L
    order = [(d, j if d == 0 else n_sc - 1 - j) for j in range(n_sc) for d in range(2)]
    for d, j in order:
        _, q_ref, kt_ref, v_ref, gm_ref, gb_ref, gr_ref, o_ref = dirs[d]
        tsl = slice(j * L, (j + 1) * L)
        causal = (col <= row) if d == 0 else (col >= row)
        last = L - 1 if d == 0 else 0
        m_prev_v = m_ref[d:d + 1, :]
        m_t = jnp.maximum(gm_ref[tsl, :], m_prev_v)
        w_inter_t = jnp.exp(m_prev_v - m_t)
        gb_t = gb_ref[tsl, :]
        den_min_t = jnp.exp(-(gb_t + m_t))
        m_last_v = m_t[last:last + 1, :]
        decay_v = jnp.exp(m_prev_v - m_last_v)
        m_ref[d:d + 1, :] = gb_t[last:last + 1, :] + m_last_v
        heads = range(MLSTM_HEADS)
        rs = [d * MLSTM_HEADS + h for h in heads]
        qs = [q_ref[tsl, h * dk:(h + 1) * dk] for h in heads]
        kts = [kt_ref[h * dk:(h + 1) * dk, tsl] for h in heads]
        v_augs = [jnp.concatenate([v_ref[tsl, h * dv:(h + 1) * dv], _ones_column_block(L, LANES, rs[h])], axis=1)
                  for h in heads]
        a_rs = [gr_ref[r:r + 1, tsl] for r in rs]
        c_prevs = [c_ref[r] for r in rs]
        w_intras = [jnp.exp(jnp.where(causal, a_rs[h] - m_t[:, rs[h]:rs[h] + 1], -jnp.inf)) for h in heads]
        s_raws = [_dot(qs[h], kts[h]) for h in heads]
        scs = [(s_raws[h] * w_intras[h]).astype(BF16) for h in heads]
        q_inters = [(qs[h].astype(F32) * w_inter_t[:, rs[h]:rs[h] + 1]).astype(BF16) for h in heads]
        tots = [_dot(jnp.concatenate([scs[h], q_inters[h]], axis=1),
                     jnp.concatenate([v_augs[h], c_prevs[h].astype(BF16)], axis=0)) for h in heads]
        for h in heads:
            r = rs[h]
            den = jnp.maximum(jnp.abs(tots[h][:, dv:]), den_min_t)
            o_ref[tsl, h * dv:(h + 1) * dv] = (tots[h][:, :dv] * (1.0 / den)[:, r:r + 1]).astype(BF16)
        for h in heads:
            r = rs[h]
            w_k = jnp.exp(a_rs[h] - m_last_v[:, r:r + 1])
            kw = (kts[h].astype(F32) * w_k).astype(BF16)
            c_ref[r] = decay_v[:, r:r + 1] * c_prevs[h] + _dot(kw, v_augs[h])


def _mlstm(mq, mkt, mv, gcm, gb, grow):
    b, s, _ = mq.shape
    L = SCAN_CHUNK * SCAN_STEP_CHUNKS
    assert s % L == 0, (s, L)
    n_chunk = s // L
    fwd = lambda bb, c: (bb, c, 0)
    bwd = lambda bb, c: (bb, n_chunk - 1 - c, 0)
    fwd_t = lambda bb, c: (bb, 0, c)
    bwd_t = lambda bb, c: (bb, 0, n_chunk - 1 - c)

    def dir_specs(tok, tr):
        return [
            pl.BlockSpec((None, L, M_QK_W), tok),
            pl.BlockSpec((None, M_QK_W, L), tr),
            pl.BlockSpec((None, L, M_V_W), tok),
            pl.BlockSpec((None, L, LANES), tok),
            pl.BlockSpec((None, L, LANES), tok),
            pl.BlockSpec((None, N_STATE, L), tr),
        ]

    return pl.pallas_call(
        _mlstm_kernel,
        grid=(b, n_chunk),
        in_specs=dir_specs(fwd, fwd_t) + dir_specs(bwd, bwd_t),
        out_specs=[pl.BlockSpec((None, L, M_V_W), fwd), pl.BlockSpec((None, L, M_V_W), bwd)],
        out_shape=[jax.ShapeDtypeStruct((b, s, M_V_W), BF16)] * 2,
        scratch_shapes=[
            pltpu.VMEM((N_STATE, MLSTM_QK_DIM, V_AUG), F32),
            pltpu.VMEM((SUBLANES, LANES), F32),
        ],
        compiler_params=pltpu.CompilerParams(
            dimension_semantics=("arbitrary", "arbitrary"), vmem_limit_bytes=VMEM_LIMIT),
        name="mlstm",
    )(mq, mkt, mv, gcm, gb, grow, mq, mkt, mv, gcm, gb, grow)


def _merge_kernel(sub, h_ref, oa_ref, hf_ref, hb_ref, gpre_ref, wmo_ref, wbg_ref, wba_ref, wbm_ref, wout_ref,
                  gm_ref, gpost_ref, o_ref):
    dv = MLSTM_V_DIM
    n_sub = h_ref.shape[0] // sub
    rows = [slice(s * sub, (s + 1) * sub) for s in range(n_sub)]

    def prologue(s):
        u = _rms(h_ref[rows[s], :], gpre_ref[...]).astype(BF16)
        hm = hf_ref[rows[s], :].astype(F32) + hb_ref[rows[s], :].astype(F32)
        hm = jnp.concatenate(
            [hm[:, i * dv:(i + 1) * dv]
             * lax.rsqrt(jnp.mean(hm[:, i * dv:(i + 1) * dv] ** 2, axis=-1, keepdims=True) + RMS_EPS)
             for i in range(MLSTM_HEADS)], axis=-1)
        return u, hm * gm_ref[...]

    subs = range(n_sub)
    pre = [prologue(s) for s in subs]
    us = [p[0] for p in pre]
    hms = [(pre[s][1] * _sigmoid(_dot(us[s], wmo_ref[...]))).astype(BF16) for s in subs]
    yms = [_dot(hms[s], wbm_ref[...]) for s in subs]
    yas = [_dot(oa_ref[rows[s], :], wba_ref[...]) for s in subs]
    merged = [(_sigmoid(_dot(us[s], wbg_ref[:, :D_MODEL])) * yas[s]
               + _sigmoid(_dot(us[s], wbg_ref[:, D_MODEL:])) * yms[s]).astype(BF16) for s in subs]
    mixes = [_dot(merged[s], wout_ref[...]) for s in subs]
    for s in subs:
        o_ref[rows[s], :] = h_ref[rows[s], :] + _rms(mixes[s], gpost_ref[...])


def _merge(h, oa, hf, hb, g_pre, w_mo, w_bg, w_ba, w_bm, w_out, g_m, g_post, *, tm=1024, sub=256):
    t, d = h.shape
    assert t % tm == 0 and tm % sub == 0, (t, tm, sub)
    row = lambda i: (i, 0)
    full = lambda a: _resident(a.shape)
    return pl.pallas_call(
        functools.partial(_merge_kernel, sub),
        grid=(t // tm,),
        in_specs=[
            pl.BlockSpec((tm, d), row), pl.BlockSpec((tm, ATT_Q_W), row),
            pl.BlockSpec((tm, M_V_W), row), pl.BlockSpec((tm, M_V_W), row),
            full(g_pre), full(w_mo), full(w_bg), full(w_ba), full(w_bm), full(w_out), full(g_m), full(g_post),
        ],
        out_specs=pl.BlockSpec((tm, d), row),
        out_shape=jax.ShapeDtypeStruct((t, d), F32),
        compiler_params=pltpu.CompilerParams(
            dimension_semantics=("arbitrary",), vmem_limit_bytes=VMEM_LIMIT),
        name="merge",
    )(h, oa, hf, hb, g_pre, w_mo, w_bg, w_ba, w_bm, w_out, g_m, g_post)


def _rope_tables(seq):
    pos = jnp.arange(seq, dtype=F32)
    inv = ROPE_THETA ** (-jnp.arange(0, ROPE_DIM, 2, dtype=F32) / ROPE_DIM)
    ang = pos[:, None] * inv[None, :]
    cos, sin = jnp.cos(ang), jnp.sin(ang)
    half = ROPE_DIM // 2
    dpos = jnp.arange(LANES) % ATT_HEAD_DIM
    sel = dpos % half
    rc = jnp.where(dpos < ROPE_DIM, cos[:, sel], 1.0)
    rs1 = jnp.where(dpos < half, -sin[:, sel], 0.0)
    rs2 = jnp.where((dpos >= half) & (dpos < ROPE_DIM), sin[:, sel], 0.0)
    return rc, rs1, rs2


def _gate_columns(a):
    g = a.reshape(a.shape[:-1] + (4, MLSTM_HEADS))
    g = jnp.stack([g[..., 0, :], g[..., 2, :], g[..., 1, :], g[..., 3, :]], axis=-2)
    g = g.reshape(a.shape[:-1] + (M_GATE_W,))
    return jnp.pad(g, [(0, 0)] * (a.ndim - 1) + [(0, LANES - M_GATE_W)])


def kernel(x, p, ffn1_norm_pre, ffn1_w1, ffn1_w2, ffn1_norm_post, mix_norm_pre, w_in, b_gates, conv_w, conv_b,
           attn_sink, mlstm_norm, w_branch_attn, w_branch_mlstm, w_out, mix_norm_post, ffn2_norm_pre, ffn2_w1,
           ffn2_w2, ffn2_norm_post, ple_norm_pre, w_ple_gate, w_ple_proj, ple_norm_post):
    bsz, seq, d = x.shape
    depth = p.shape[0]
    h = x.reshape(bsz * seq, d)
    rope_tabs = _rope_tables(seq)
    bf = lambda a: a.astype(BF16)
    whole = lambda a: [(0, a.shape[1])]
    for i in range(depth):
        wi = w_in[i]
        later = [wi, ffn2_w1[i], ffn2_w2[i], w_branch_attn[i], w_branch_mlstm[i], w_out[i], w_ple_gate[i],
                 w_ple_proj[i]]
        cols = [[(0, OFF_MO), (OFF_MO, OFF_MG), (OFF_BG, IN_WIDTH)]] + [whole(a) for a in later[1:]]
        h, (w_a, w_mo, w_bg, w21, w22, w_ba, w_bm, w_o, w_pg, w_pp) = _ffn(
            h, ffn1_norm_pre[i][None], bf(ffn1_w1[i]), bf(ffn1_w2[i]), ffn1_norm_post[i][None],
            casts=list(zip(later, cols)), name="ffn1")
        w_g = bf(_gate_columns(wi[:, OFF_MG:OFF_BG]))
        b_gate = _gate_columns(b_gates[i])[None]
        q, k, v, mq, mkt, mv, gcm, gb, grow = _inproj(
            h.reshape(bsz, seq, d), mix_norm_pre[i][None], w_a, w_g, b_gate, rope_tabs, conv_w[i], conv_b[i][None])
        oa = _attention(q, k, v, attn_sink[i])
        hf, hb = _mlstm(mq, mkt, mv, gcm, gb, grow)
        h = _merge(h, oa.reshape(bsz * seq, ATT_Q_W), hf.reshape(bsz * seq, M_V_W), hb.reshape(bsz * seq, M_V_W),
                   mix_norm_pre[i][None], w_mo, w_bg, w_ba, w_bm, w_o, mlstm_norm[i][None], mix_norm_post[i][None])
        h, _ = _ffn(h, ffn2_norm_pre[i][None], w21, w22, ffn2_norm_post[i][None],
                    ple=(p.reshape(depth * bsz * seq, PLE_DIM), i * bsz * seq, ple_norm_pre[i][None], w_pg, w_pp,
                         ple_norm_post[i][None]), name="ffn2_ple")
    return h.reshape(bsz, seq, d)
```

```python
import functools

import jax
import jax.numpy as jnp
from jax import lax
from jax.experimental import pallas as pl
from jax.experimental.pallas import tpu as pltpu

D_MODEL = 1024
PLE_DIM = 256
RMS_EPS = 1e-6
MACARON = 0.5
FFN_HIDDEN = 2816
ATT_HEADS = 8
ATT_KV_HEADS = 2
ATT_HEAD_DIM = 64
ATT_WINDOW = 128
ATT_BLOCK = 128
ROPE_THETA = 500000.0
ROPE_DIM = ATT_HEAD_DIM // 4
MLSTM_HEADS = 4
MLSTM_QK_DIM = 128
MLSTM_V_DIM = 256
MLSTM_CONV = 5
ATT_Q_W = ATT_HEADS * ATT_HEAD_DIM
ATT_KV_W = ATT_KV_HEADS * ATT_HEAD_DIM
M_QK_W = MLSTM_HEADS * MLSTM_QK_DIM
M_V_W = MLSTM_HEADS * MLSTM_V_DIM
M_GATE_W = 4 * MLSTM_HEADS

OFF_AQ = 0
OFF_AK = OFF_AQ + ATT_Q_W
OFF_AV = OFF_AK + ATT_KV_W
OFF_MQ = OFF_AV + ATT_KV_W
OFF_MV = OFF_MQ + 2 * M_QK_W
OFF_MO = OFF_MV + M_V_W
OFF_MG = OFF_MO + M_V_W
OFF_BG = OFF_MG + M_GATE_W
IN_WIDTH = OFF_BG + 2 * D_MODEL

LANES = 128
SUBLANES = 8
BF16_ROWS = 16
VMEM_LIMIT = 56 * 1024 * 1024

SCAN_CHUNK = 256
SCAN_STEP_CHUNKS = 4
N_STATE = 2 * MLSTM_HEADS
V_AUG = MLSTM_V_DIM + LANES
CONV_HALO = BF16_ROWS

LOG2E = 1.4426950408889634
ATT_UNIT_SUBS = 16

BF16 = jnp.bfloat16
F32 = jnp.float32


def _rms(x, g):
    return x * lax.rsqrt(jnp.mean(x * x, axis=-1, keepdims=True) + RMS_EPS) * g


def _sigmoid(x):
    return 1.0 / (1.0 + jnp.exp(-x))


def _dot(a, b):
    return jnp.dot(a, b, preferred_element_type=F32)


def _dot_nt(a, b):
    return lax.dot_general(a, b, (((1,), (1,)), ((), ())), preferred_element_type=F32)


def _ones_column_block(rows, width, col=0):
    return jnp.where(lax.broadcasted_iota(jnp.int32, (rows, width), 1) == col, 1.0, 0.0).astype(BF16)


def _ffn_kernel(sub, hid_chunk, with_ple, cast_cols, *refs):
    n_main = 10 if with_ple else 5
    n_cast = len(cast_cols)
    if with_ple:
        (h_ref, gpre_ref, w1_ref, w2_ref, gpost_ref, p_ref, gple_pre_ref, wpg_ref, wpp_ref,
         gple_post_ref) = refs[:n_main]
    else:
        h_ref, gpre_ref, w1_ref, w2_ref, gpost_ref = refs[:n_main]
    cast_in = refs[n_main:n_main + n_cast]
    o_ref = refs[n_main + n_cast]
    cast_out = iter(refs[n_main + n_cast + 1:])
    for src_ref, cols in zip(cast_in, cast_cols):
        for c0, c1 in cols:
            next(cast_out)[...] = src_ref[:, c0:c1].astype(BF16)

    n_sub = h_ref.shape[0] // sub
    rows = [slice(s * sub, (s + 1) * sub) for s in range(n_sub)]
    xn = [_rms(h_ref[r, :], gpre_ref[...]).astype(BF16) for r in rows]

    def main(s):
        acc = None
        for c in range(0, FFN_HIDDEN, hid_chunk):
            z1 = _dot(xn[s], w1_ref[:, c:c + hid_chunk])
            z2 = _dot(xn[s], w1_ref[:, FFN_HIDDEN + c:FFN_HIDDEN + c + hid_chunk])
            part = _dot((z1 * _sigmoid(z1) * z2).astype(BF16), w2_ref[c:c + hid_chunk, :])
            acc = part if acc is None else acc + part
        return acc

    def epilogue(s, acc):
        h = h_ref[rows[s], :] + MACARON * _rms(acc, gpost_ref[...])
        if with_ple:
            gate = _sigmoid(_dot(_rms(h, gple_pre_ref[...]).astype(BF16), wpg_ref[...]))
            proj = _dot(p_ref[rows[s], :].astype(BF16), wpp_ref[...])
            h = h + _rms(proj * gate, gple_post_ref[...])
        o_ref[rows[s], :] = h

    prev = main(0)
    for s in range(1, n_sub):
        cur = main(s)
        epilogue(s - 1, prev)
        prev = cur
    epilogue(n_sub - 1, prev)


def _resident(shape):
    return pl.BlockSpec(shape, lambda *_: (0,) * len(shape), pipeline_mode=pl.Buffered(1))


def _ffn(h, g_pre, w1, w2, g_post, ple=None, casts=(), *, tm=1024, sub=256, hid_chunk=2816, name="ffn"):
    t, d = h.shape
    assert t % tm == 0 and tm % sub == 0, (t, tm, sub)
    n_step = t // tm
    row = lambda i: (i, 0)
    in_specs = [pl.BlockSpec((tm, d), row), _resident((1, d)), _resident(w1.shape), _resident(w2.shape),
                _resident((1, d))]
    args = [h, g_pre, w1, w2, g_post]
    if ple is not None:
        p, p_row0, g_ple_pre, w_gate, w_proj, g_ple_post = ple
        in_specs += [pl.BlockSpec((tm, PLE_DIM), lambda i: (i + p_row0 // tm, 0)), _resident((1, d)),
                     _resident(w_gate.shape),
                     _resident(w_proj.shape), _resident((1, d))]
        args += [p, g_ple_pre, w_gate, w_proj, g_ple_post]
    out_specs = [pl.BlockSpec((tm, d), row)]
    out_shape = [jax.ShapeDtypeStruct((t, d), F32)]
    for src, cols in casts:
        n_rows = src.shape[0]
        n_blk = max(nb for nb in range(1, n_step + 1)
                    if n_step % nb == 0 and n_rows % (nb * BF16_ROWS) == 0)
        blk_map = functools.partial(lambda spb, i: (i // spb, 0), n_step // n_blk)
        in_specs.append(pl.BlockSpec((n_rows // n_blk, src.shape[1]), blk_map))
        args.append(src)
        for c0, c1 in cols:
            out_specs.append(pl.BlockSpec((n_rows // n_blk, c1 - c0), blk_map))
            out_shape.append(jax.ShapeDtypeStruct((n_rows, c1 - c0), BF16))
    outs = pl.pallas_call(
        functools.partial(_ffn_kernel, sub, hid_chunk, ple is not None, tuple(tuple(c) for _, c in casts)),
        grid=(n_step,),
        in_specs=in_specs,
        out_specs=out_specs,
        out_shape=out_shape,
        compiler_params=pltpu.CompilerParams(dimension_semantics=("arbitrary",), vmem_limit_bytes=VMEM_LIMIT),
        name=name,
    )(*args)
    return outs[0], list(outs[1:])


WA_Q = 0
WA_KV = WA_Q + ATT_Q_W
WA_MQK = WA_KV + 2 * ATT_KV_W
WA_MV = WA_MQK + 2 * M_QK_W
WA_END = WA_MV + M_V_W


def _lane_scan(x, op, ident, reverse):
    width = x.shape[1]
    lane = lax.broadcasted_iota(jnp.int32, x.shape, 1)
    s = 1
    while s < width:
        if reverse:
            shifted = jnp.where(lane < width - s, pltpu.roll(x, width - s, 1), ident)
        else:
            shifted = jnp.where(lane >= s, pltpu.roll(x, s, 1), ident)
        x = op(x, shifted)
        s *= 2
    return x


def _inproj_kernel(n_tile, sub, h_ref, hp_ref, hn_ref, g_ref, w_ref, wg_ref, bg_ref, rc_ref, rs1_ref, rs2_ref, cw_ref,
                   cb_ref, q_ref, k_ref, v_ref, mq_ref, mkt_ref, mv_ref, gcm_ref, gb_ref, grow_ref, ue_ref, ze_ref):
    i = pl.program_id(0)
    tm = h_ref.shape[0]
    g_pre = g_ref[...]
    halo = CONV_HALO
    ue_ref[0:halo, :] = jnp.where(i > 0, _rms(hp_ref[...], g_pre), 0.0).astype(BF16)
    ue_ref[halo:halo + tm, :] = _rms(h_ref[...], g_pre).astype(BF16)
    ue_ref[halo + tm:, :] = jnp.where(i < n_tile - 1, _rms(hn_ref[...], g_pre), 0.0).astype(BF16)
    is_fwd = lax.broadcasted_iota(jnp.int32, (N_STATE, SCAN_CHUNK), 0) < MLSTM_HEADS
    cat = lambda parts: jnp.concatenate(parts, axis=1)

    n_sub = tm // sub
    rows = [slice(s * sub, (s + 1) * sub) for s in range(n_sub)]
    u_of = lambda s: ue_ref[halo + s * sub:halo + (s + 1) * sub, :]

    def attn_proj(s):
        rc, rs1, rs2 = rc_ref[rows[s], :], rs1_ref[rows[s], :], rs2_ref[rows[s], :]

        def rope(x):
            half = ROPE_DIM // 2
            return x * rc + pltpu.roll(x, LANES - half, 1) * rs1 + pltpu.roll(x, half, 1) * rs2

        zq = _dot(u_of(s), w_ref[:, WA_Q:WA_KV])
        for c in range(ATT_Q_W // LANES):
            q_ref[rows[s], c * LANES:(c + 1) * LANES] = (
                rope(zq[:, c * LANES:(c + 1) * LANES]) * (ATT_HEAD_DIM ** -0.5 * LOG2E)).astype(BF16)
        zkv = _dot(u_of(s), w_ref[:, WA_KV:WA_MQK])
        k_ref[rows[s], :] = rope(zkv[:, :ATT_KV_W]).astype(BF16)
        v_ref[rows[s], :] = zkv[:, ATT_KV_W:].astype(BF16)

    def value_proj(s):
        mv_ref[rows[s], :] = _dot(u_of(s), w_ref[:, WA_MV:WA_END]).astype(BF16)

    def qk_proj(s):
        ze_ref[s] = _dot(ue_ref[s * sub:(s + 1) * sub + 2 * halo, :], w_ref[:, WA_MQK:WA_MV])

    def qk_conv(s):
        ze = ze_ref.at[s]
        cw = cw_ref[...]
        y = cb_ref[...]
        for j in range(MLSTM_CONV):
            off = halo - MLSTM_CONV // 2 + j
            y = y + ze[off:off + sub, :] * cw[j:j + 1, :]
        qk = y * _sigmoid(y)
        mq_ref[rows[s], :] = qk[:, :M_QK_W].astype(BF16)
        mkt_ref[:, rows[s]] = (qk[:, M_QK_W:] * (MLSTM_QK_DIM ** -0.5)).T.astype(BF16)

    for s in range(n_sub):
        attn_proj(s)
        value_proj(s)
        qk_proj(s)
        qk_conv(s)
        u = u_of(s)
        rows_s = rows[s]
        gt = (_dot(u, wg_ref[...]) + bg_ref[...]).T
        li = gt[0:N_STATE]
        f_raw = gt[N_STATE:2 * N_STATE]
        lf = jnp.minimum(f_raw, 0.0) - jnp.log1p(jnp.exp(-jnp.abs(f_raw)))
        a_parts, cm_parts, b_parts = [], [], []
        for c in range(sub // SCAN_CHUNK):
            sl = slice(c * SCAN_CHUNK, (c + 1) * SCAN_CHUNK)
            lfc = lf[:, sl]
            b = jnp.where(is_fwd, _lane_scan(lfc, jnp.add, 0.0, False), _lane_scan(lfc, jnp.add, 0.0, True))
            a = li[:, sl] - b
            cm = jnp.where(is_fwd, _lane_scan(a, jnp.maximum, -jnp.inf, False),
                           _lane_scan(a, jnp.maximum, -jnp.inf, True))
            a_parts.append(a)
            cm_parts.append(cm)
            b_parts.append(b)
        grow_ref[:, rows_s] = cat(a_parts)
        pad = jnp.zeros((LANES - N_STATE, sub), F32)
        gcm_ref[rows_s, :] = jnp.concatenate([cat(cm_parts), pad], axis=0).T
        gb_ref[rows_s, :] = jnp.concatenate([cat(b_parts), pad], axis=0).T


def _inproj(h3, g_pre, w_a, w_g, b_gate, rope_tabs, conv_w, conv_b, *, tm=1024, sub=512):
    b, s, d = h3.shape
    assert s % tm == 0 and tm % sub == 0 and sub % SCAN_CHUNK == 0, (s, tm, sub)
    n_tile = s // tm
    hpt = tm // CONV_HALO
    n_halo = s // CONV_HALO
    tok = lambda i, bb: (bb, i, 0)
    tok_p = lambda i, bb: (bb, jnp.maximum(i * hpt - 1, 0), 0)
    tok_n = lambda i, bb: (bb, jnp.minimum((i + 1) * hpt, n_halo - 1), 0)
    tab = lambda i, bb: (i, 0)
    qk_w = 2 * M_QK_W
    tok_spec = lambda w: pl.BlockSpec((None, tm, w), tok)
    tr_spec = lambda r: pl.BlockSpec((None, r, tm), lambda i, bb: (bb, 0, i))
    return pl.pallas_call(
        functools.partial(_inproj_kernel, n_tile, sub),
        grid=(n_tile, b),
        in_specs=[
            tok_spec(d),
            pl.BlockSpec((None, CONV_HALO, d), tok_p),
            pl.BlockSpec((None, CONV_HALO, d), tok_n),
            _resident((1, d)),
            _resident((d, WA_END)),
            _resident((d, LANES)),
            _resident((1, LANES)),
            pl.BlockSpec((tm, LANES), tab),
            pl.BlockSpec((tm, LANES), tab),
            pl.BlockSpec((tm, LANES), tab),
            _resident((MLSTM_CONV, qk_w)),
            _resident((1, qk_w)),
        ],
        out_specs=[tok_spec(ATT_Q_W), tok_spec(ATT_KV_W), tok_spec(ATT_KV_W), tok_spec(M_QK_W), tr_spec(M_QK_W),
                   tok_spec(M_V_W), tok_spec(LANES), tok_spec(LANES), tr_spec(N_STATE)],
        out_shape=[
            jax.ShapeDtypeStruct((b, s, ATT_Q_W), BF16),
            jax.ShapeDtypeStruct((b, s, ATT_KV_W), BF16),
            jax.ShapeDtypeStruct((b, s, ATT_KV_W), BF16),
            jax.ShapeDtypeStruct((b, s, M_QK_W), BF16),
            jax.ShapeDtypeStruct((b, M_QK_W, s), BF16),
            jax.ShapeDtypeStruct((b, s, M_V_W), BF16),
            jax.ShapeDtypeStruct((b, s, LANES), F32),
            jax.ShapeDtypeStruct((b, s, LANES), F32),
            jax.ShapeDtypeStruct((b, N_STATE, s), F32),
        ],
        scratch_shapes=[pltpu.VMEM((tm + 2 * CONV_HALO, d), BF16),
                        pltpu.VMEM((tm // sub, sub + 2 * CONV_HALO, qk_w), F32)],
        compiler_params=pltpu.CompilerParams(
            dimension_semantics=("arbitrary", "arbitrary"), vmem_limit_bytes=VMEM_LIMIT),
        name="inproj",
    )(h3, h3, h3, g_pre, w_a, w_g, b_gate, *rope_tabs, conv_w, conv_b)


def _attn_kernel(sink_ref, q_ref, kp_ref, kc_ref, kn_ref, vp_ref, vc_ref, vn_ref, o_ref):
    i = pl.program_id(1)
    blk = ATT_BLOCK
    tq = q_ref.shape[0]
    dh = ATT_HEAD_DIM
    group = ATT_HEADS // ATT_KV_HEADS
    rows = group * blk
    n_sub = tq // blk
    qi = lax.broadcasted_iota(jnp.int32, (rows, blk), 0) % blk
    kc = lax.broadcasted_iota(jnp.int32, (rows, blk), 1)
    bias_prev = jnp.where(kc >= qi, 0.0, -jnp.inf)
    bias_next = jnp.where(kc <= qi, 0.0, -jnp.inf)
    first_prev = jnp.where(i == 0, -jnp.inf, bias_prev)
    last_next = jnp.where(i == pl.num_programs(1) - 1, -jnp.inf, bias_next)
    head_of_row = lax.broadcasted_iota(jnp.int32, (rows, 1), 0) // blk
    kcat = jnp.concatenate([kp_ref[...], kc_ref[...], kn_ref[...]], axis=0)
    vcat = jnp.concatenate([vp_ref[...], vc_ref[...], vn_ref[...]], axis=0)
    ones_blk = _ones_column_block(3 * blk, dh)
    sinks = []
    for g in range(ATT_KV_HEADS):
        sink = jnp.zeros((rows, 1), F32)
        for hh in range(group):
            sink = jnp.where(head_of_row == hh, sink_ref[g * group + hh] * LOG2E, sink)
        sinks.append(sink)
    def scores(sb, g):
        qs = jnp.concatenate(
            [q_ref[sb * blk:(sb + 1) * blk, (g * group + hh) * dh:(g * group + hh + 1) * dh]
             for hh in range(group)], axis=0)
        s = _dot_nt(qs, kcat[sb * blk:(sb + 3) * blk, g * dh:(g + 1) * dh])
        bp = first_prev if sb == 0 else bias_prev
        bn = last_next if sb == n_sub - 1 else bias_next
        return jnp.concatenate([s[:, :blk] + bp, s[:, blk:2 * blk], s[:, 2 * blk:] + bn], axis=1)

    units = [(sb, g) for sb in range(n_sub) for g in range(ATT_KV_HEADS)]
    ss = [scores(sb, g) for sb, g in units]
    mxs = [jnp.maximum(jnp.max(s, axis=-1, keepdims=True), sinks[g]) for s, (sb, g) in zip(ss, units)]
    pvs = [_dot(jnp.exp2((s - mx).astype(BF16)),
                jnp.concatenate([vcat[sb * blk:(sb + 3) * blk, g * dh:(g + 1) * dh], ones_blk], axis=1))
           for s, mx, (sb, g) in zip(ss, mxs, units)]
    os_ = [pv[:, :dh] / (pv[:, dh:dh + 1] + jnp.exp2(sinks[g] - mx)) for pv, mx, (sb, g) in zip(pvs, mxs, units)]
    for sb in range(n_sub):
        outs = [os_[sb * ATT_KV_HEADS + g][hh * blk:(hh + 1) * blk]
                for g in range(ATT_KV_HEADS) for hh in range(group)]
        o_ref[sb * blk:(sb + 1) * blk, :] = jnp.concatenate(outs, axis=-1).astype(BF16)


def _attention(q, k, v, sink, *, tq=ATT_UNIT_SUBS * ATT_BLOCK):
    b, s, _ = q.shape
    assert s % tq == 0 and tq % ATT_BLOCK == 0, (s, tq)
    n_tile = s // tq
    bpt = tq // ATT_BLOCK
    n_blk = s // ATT_BLOCK
    cur = lambda bb, i: (bb, i, 0)
    prev = lambda bb, i: (bb, jnp.maximum(i * bpt - 1, 0), 0)
    nxt = lambda bb, i: (bb, jnp.minimum((i + 1) * bpt, n_blk - 1), 0)
    halo_spec = lambda im: pl.BlockSpec((None, ATT_BLOCK, ATT_KV_W), im)
    main_spec = pl.BlockSpec((None, tq, ATT_KV_W), cur)
    return pl.pallas_call(
        _attn_kernel,
        grid=(b, n_tile),
        in_specs=[
            pl.BlockSpec(memory_space=pltpu.SMEM),
            pl.BlockSpec((None, tq, ATT_Q_W), cur),
            halo_spec(prev), main_spec, halo_spec(nxt),
            halo_spec(prev), main_spec, halo_spec(nxt),
        ],
        out_specs=pl.BlockSpec((None, tq, ATT_Q_W), cur),
        out_shape=jax.ShapeDtypeStruct((b, s, ATT_Q_W), BF16),
        compiler_params=pltpu.CompilerParams(
            dimension_semantics=("arbitrary", "arbitrary"), vmem_limit_bytes=VMEM_LIMIT),
        name="attn",
    )(sink, q, k, k, k, v, v, v)


def _mlstm_kernel(qf_ref, ktf_ref, vf_ref, gmf_ref, gbf_ref, grf_ref,
                  qb_ref, ktb_ref, vb_ref, gmb_ref, gbb_ref, grb_ref,
                  of_ref, ob_ref, c_ref, m_ref):
    c = pl.program_id(1)
    L = SCAN_CHUNK
    dk = MLSTM_QK_DIM
    dv = MLSTM_V_DIM

    @pl.when(c == 0)
    def _():
        c_ref[...] = jnp.zeros_like(c_ref)
        m_ref[...] = jnp.zeros_like(m_ref)

    row = lax.broadcasted_iota(jnp.int32, (L, L), 0)
    col = lax.broadcasted_iota(jnp.int32, (L, L), 1)
    dirs = (
        (0, qf_ref, ktf_ref, vf_ref, gmf_ref, gbf_ref, grf_ref, of_ref),
        (1, qb_ref, ktb_ref, vb_ref, gmb_ref, gbb_ref, grb_ref, ob_ref),
    )
    n_sc = qf_ref.shape[0] // L
    order = [(d, j if d == 0 else n_sc - 1 - j) for j in range(n_sc) for d in range(2)]
    for d, j in order:
        _, q_ref, kt_ref, v_ref, gm_ref, gb_ref, gr_ref, o_ref = dirs[d]
        tsl = slice(j * L, (j + 1) * L)
        causal = (col <= row) if d == 0 else (col >= row)
        last = L - 1 if d == 0 else 0
        m_prev_v = m_ref[d:d + 1, :]
        m_t = jnp.maximum(gm_ref[tsl, :], m_prev_v)
        w_inter_t = jnp.exp(m_prev_v - m_t)
        gb_t = gb_ref[tsl, :]
        den_min_t = jnp.exp(-(gb_t + m_t))
        m_last_v = m_t[last:last + 1, :]
        decay_v = jnp.exp(m_prev_v - m_last_v)
        m_ref[d:d + 1, :] = gb_t[last:last + 1, :] + m_last_v
        heads = range(MLSTM_HEADS)
        rs = [d * MLSTM_HEADS + h for h in heads]
        qs = [q_ref[tsl, h * dk:(h + 1) * dk] for h in heads]
        kts = [kt_ref[h * dk:(h + 1) * dk, tsl] for h in heads]
        v_augs = [jnp.concatenate([v_ref[tsl, h * dv:(h + 1) * dv], _ones_column_block(L, LANES, rs[h])], axis=1)
                  for h in heads]
        a_rs = [gr_ref[r:r + 1, tsl] for r in rs]
        c_prevs = [c_ref[r] for r in rs]
        w_intras = [jnp.exp(jnp.where(causal, a_rs[h] - m_t[:, rs[h]:rs[h] + 1], -jnp.inf)) for h in heads]
        s_raws = [_dot(qs[h], kts[h]) for h in heads]
        scs = [(s_raws[h] * w_intras[h]).astype(BF16) for h in heads]
        q_inters = [(qs[h].astype(F32) * w_inter_t[:, rs[h]:rs[h] + 1]).astype(BF16) for h in heads]
        tots = [_dot(jnp.concatenate([scs[h], q_inters[h]], axis=1),
                     jnp.concatenate([v_augs[h], c_prevs[h].astype(BF16)], axis=0)) for h in heads]
        for h in heads:
            r = rs[h]
            den = jnp.maximum(jnp.abs(tots[h][:, dv:]), den_min_t)
            o_ref[tsl, h * dv:(h + 1) * dv] = (tots[h][:, :dv] * (1.0 / den)[:, r:r + 1]).astype(BF16)
        for h in heads:
            r = rs[h]
            w_k = jnp.exp(a_rs[h] - m_last_v[:, r:r + 1])
            kw = (kts[h].astype(F32) * w_k).astype(BF16)
            c_ref[r] = decay_v[:, r:r + 1] * c_prevs[h] + _dot(kw, v_augs[h])


def _mlstm(mq, mkt, mv, gcm, gb, grow):
    b, s, _ = mq.shape
    L = SCAN_CHUNK * SCAN_STEP_CHUNKS
    assert s % L == 0, (s, L)
    n_chunk = s // L
    fwd = lambda bb, c: (bb, c, 0)
    bwd = lambda bb, c: (bb, n_chunk - 1 - c, 0)
    fwd_t = lambda bb, c: (bb, 0, c)
    bwd_t = lambda bb, c: (bb, 0, n_chunk - 1 - c)

    def dir_specs(tok, tr):
        return [
            pl.BlockSpec((None, L, M_QK_W), tok),
            pl.BlockSpec((None, M_QK_W, L), tr),
            pl.BlockSpec((None, L, M_V_W), tok),
            pl.BlockSpec((None, L, LANES), tok),
            pl.BlockSpec((None, L, LANES), tok),
            pl.BlockSpec((None, N_STATE, L), tr),
        ]

    return pl.pallas_call(
        _mlstm_kernel,
        grid=(b, n_chunk),
        in_specs=dir_specs(fwd, fwd_t) + dir_specs(bwd, bwd_t),
        out_specs=[pl.BlockSpec((None, L, M_V_W), fwd), pl.BlockSpec((None, L, M_V_W), bwd)],
        out_shape=[jax.ShapeDtypeStruct((b, s, M_V_W), BF16)] * 2,
        scratch_shapes=[
            pltpu.VMEM((N_STATE, MLSTM_QK_DIM, V_AUG), F32),
            pltpu.VMEM((SUBLANES, LANES), F32),
        ],
        compiler_params=pltpu.CompilerParams(
            dimension_semantics=("arbitrary", "arbitrary"), vmem_limit_bytes=VMEM_LIMIT),
        name="mlstm",
    )(mq, mkt, mv, gcm, gb, grow, mq, mkt, mv, gcm, gb, grow)


def _merge_kernel(sub, h_ref, oa_ref, hf_ref, hb_ref, gpre_ref, wmo_ref, wbg_ref, wba_ref, wbm_ref, wout_ref,
                  gm_ref, gpost_ref, o_ref):
    dv = MLSTM_V_DIM
    n_sub = h_ref.shape[0] // sub
    rows = [slice(s * sub, (s + 1) * sub) for s in range(n_sub)]

    def prologue(s):
        u = _rms(h_ref[rows[s], :], gpre_ref[...]).astype(BF16)
        hm = hf_ref[rows[s], :].astype(F32) + hb_ref[rows[s], :].astype(F32)
        hm = jnp.concatenate(
            [hm[:, i * dv:(i + 1) * dv]
             * lax.rsqrt(jnp.mean(hm[:, i * dv:(i + 1) * dv] ** 2, axis=-1, keepdims=True) + RMS_EPS)
             for i in range(MLSTM_HEADS)], axis=-1)
        return u, hm * gm_ref[...]

    subs = range(n_sub)
    pre = [prologue(s) for s in subs]
    us = [p[0] for p in pre]
    hms = [(pre[s][1] * _sigmoid(_dot(us[s], wmo_ref[...]))).astype(BF16) for s in subs]
    yms = [_dot(hms[s], wbm_ref[...]) for s in subs]
    yas = [_dot(oa_ref[rows[s], :], wba_ref[...]) for s in subs]
    merged = [(_sigmoid(_dot(us[s], wbg_ref[:, :D_MODEL])) * yas[s]
               + _sigmoid(_dot(us[s], wbg_ref[:, D_MODEL:])) * yms[s]).astype(BF16) for s in subs]
    mixes = [_dot(merged[s], wout_ref[...]) for s in subs]
    for s in subs:
        o_ref[rows[s], :] = h_ref[rows[s], :] + _rms(mixes[s], gpost_ref[...])


def _merge(h, oa, hf, hb, g_pre, w_mo, w_bg, w_ba, w_bm, w_out, g_m, g_post, *, tm=1024, sub=256):
    t, d = h.shape
    assert t % tm == 0 and tm % sub == 0, (t, tm, sub)
    row = lambda i: (i, 0)
    full = lambda a: _resident(a.shape)
    return pl.pallas_call(
        functools.partial(_merge_kernel, sub),
        grid=(t // tm,),
        in_specs=[
            pl.BlockSpec((tm, d), row), pl.BlockSpec((tm, ATT_Q_W), row),
            pl.BlockSpec((tm, M_V_W), row), pl.BlockSpec((tm, M_V_W), row),
            full(g_pre), full(w_mo), full(w_bg), full(w_ba), full(w_bm), full(w_out), full(g_m), full(g_post),
        ],
        out_specs=pl.BlockSpec((tm, d), row),
        out_shape=jax.ShapeDtypeStruct((t, d), F32),
        compiler_params=pltpu.CompilerParams(
            dimension_semantics=("arbitrary",), vmem_limit_bytes=VMEM_LIMIT),
        name="merge",
    )(h, oa, hf, hb, g_pre, w_mo, w_bg, w_ba, w_bm, w_out, g_m, g_post)


def _rope_tables(seq):
    pos = jnp.arange(seq, dtype=F32)
    inv = ROPE_THETA ** (-jnp.arange(0, ROPE_DIM, 2, dtype=F32) / ROPE_DIM)
    ang = pos[:, None] * inv[None, :]
    cos, sin = jnp.cos(ang), jnp.sin(ang)
    half = ROPE_DIM // 2
    dpos = jnp.arange(LANES) % ATT_HEAD_DIM
    sel = dpos % half
    rc = jnp.where(dpos < ROPE_DIM, cos[:, sel], 1.0)
    rs1 = jnp.where(dpos < half, -sin[:, sel], 0.0)
    rs2 = jnp.where((dpos >= half) & (dpos < ROPE_DIM), sin[:, sel], 0.0)
    return rc, rs1, rs2


def _gate_columns(a):
    g = a.reshape(a.shape[:-1] + (4, MLSTM_HEADS))
    g = jnp.stack([g[..., 0, :], g[..., 2, :], g[..., 1, :], g[..., 3, :]], axis=-2)
    g = g.reshape(a.shape[:-1] + (M_GATE_W,))
    return jnp.pad(g, [(0, 0)] * (a.ndim - 1) + [(0, LANES - M_GATE_W)])


def kernel(x, p, ffn1_norm_pre, ffn1_w1, ffn1_w2, ffn1_norm_post, mix_norm_pre, w_in, b_gates, conv_w, conv_b,
           attn_sink, mlstm_norm, w_branch_attn, w_branch_mlstm, w_out, mix_norm_post, ffn2_norm_pre, ffn2_w1,
           ffn2_w2, ffn2_norm_post, ple_norm_pre, w_ple_gate, w_ple_proj, ple_norm_post):
    bsz, seq, d = x.shape
    depth = p.shape[0]
    h = x.reshape(bsz * seq, d)
    rope_tabs = _rope_tables(seq)
    bf = lambda a: a.astype(BF16)
    whole = lambda a: [(0, a.shape[1])]
    for i in range(depth):
        wi = w_in[i]
        later = [wi, ffn2_w1[i], ffn2_w2[i], w_branch_attn[i], w_branch_mlstm[i], w_out[i], w_ple_gate[i],
                 w_ple_proj[i]]
        cols = [[(0, OFF_MO), (OFF_MO, OFF_MG), (OFF_BG, IN_WIDTH)]] + [whole(a) for a in later[1:]]
        h, (w_a, w_mo, w_bg, w21, w22, w_ba, w_bm, w_o, w_pg, w_pp) = _ffn(
            h, ffn1_norm_pre[i][None], bf(ffn1_w1[i]), bf(ffn1_w2[i]), ffn1_norm_post[i][None],
            casts=list(zip(later, cols)), name="ffn1")
        w_g = bf(_gate_columns(wi[:, OFF_MG:OFF_BG]))
        b_gate = _gate_columns(b_gates[i])[None]
        q, k, v, mq, mkt, mv, gcm, gb, grow = _inproj(
            h.reshape(bsz, seq, d), mix_norm_pre[i][None], w_a, w_g, b_gate, rope_tabs, conv_w[i], conv_b[i][None])
        oa = _attention(q, k, v, attn_sink[i])
        hf, hb = _mlstm(mq, mkt, mv, gcm, gb, grow)
        h = _merge(h, oa.reshape(bsz * seq, ATT_Q_W), hf.reshape(bsz * seq, M_V_W), hb.reshape(bsz * seq, M_V_W),
                   mix_norm_pre[i][None], w_mo, w_bg, w_ba, w_bm, w_o, mlstm_norm[i][None], mix_norm_post[i][None])
        h, _ = _ffn(h, ffn2_norm_pre[i][None], w21, w22, ffn2_norm_post[i][None],
                    ple=(p.reshape(depth * bsz * seq, PLE_DIM), i * bsz * seq, ple_norm_pre[i][None], w_pg, w_pp,
                         ple_norm_post[i][None]), name="ffn2_ple")
    return h.reshape(bsz, seq, d)
```
